```python
import math
import jax, jax.numpy as jnp
from jax import lax
import numpy as np

D_MODEL = 1024
BATCH = 16
SEQ = 2048
DEPTH = 2
DEC_BATCH = 8
DEC_SEQ = 16
PAST_LEN = 1024

CHUNK = 64
D_MIX = 2 * D_MODEL
D_A = D_MIX // 2
DN_DK = 128
DN_DV = 128
DN_HEADS = D_A // DN_DV
CONV_W = 4
D_B = D_MIX // 4
SSM_GROUP = 16
SSM_GROUPS = D_B // SSM_GROUP
SSM_STATE = 64
D_C = D_MIX // 4
ML_DH = 128
ML_HEADS = D_C // ML_DH
EPS = 1e-6

PROJ_SIZES = (3 * D_A, D_A, DN_HEADS, DN_HEADS,
              D_B, D_B,
              D_C, D_C, D_C, D_C, D_C, ML_HEADS, ML_HEADS)
D_PROJ = sum(PROJ_SIZES)
PROJ_SPLITS = tuple(int(s) for s in np.cumsum(PROJ_SIZES)[:-1])

kernel_name = 'hybrid_deltanet_s5_mlstm_stream_step'


def rms_norm(x, g):
    xf = x.astype(jnp.float32)
    y = xf * lax.rsqrt(jnp.mean(xf * xf, axis=-1, keepdims=True) + EPS)
    return (y * g.astype(jnp.float32)).astype(x.dtype)


def l2norm(x):
    return x * lax.rsqrt(jnp.sum(x * x, axis=-1, keepdims=True) + EPS)


def causal_conv(x, buf, w):
    t = x.shape[1]
    xp = jnp.concatenate([buf.astype(x.dtype), x], axis=1)
    w = w.astype(x.dtype)
    y = xp[:, 0:t] * w[0]
    for j in range(1, CONV_W):
        y = y + xp[:, j:j + t] * w[j]
    return y, xp[:, -(CONV_W - 1):]


def _blk(a, nc, L):
    b, h = a.shape[0], a.shape[2]
    a = a.reshape((b, nc, L, h) + a.shape[3:])
    return jnp.moveaxis(a, 3, 1)


def gated_delta_rule(q, k, v, beta, log_alpha, s0):
    bsz, t, nh, dk = q.shape
    L = min(CHUNK, t)
    nc = t // L
    q, k, v = _blk(q, nc, L), _blk(k, nc, L), _blk(v, nc, L)
    beta = _blk(beta, nc, L)
    g = jnp.cumsum(_blk(log_alpha, nc, L), axis=-1)
    incl = jnp.tril(jnp.ones((L, L), dtype=bool))
    strict = jnp.tril(jnp.ones((L, L), dtype=bool), -1)
    gdiff = g[..., :, None] - g[..., None, :]
    decay = jnp.where(incl, jnp.exp(jnp.where(incl, gdiff, 0.0)), 0.0)
    a_mat = jnp.where(strict, beta[..., None] * jnp.einsum('bhnid,bhnjd->bhnij', k, k) * decay, 0.0)
    rhs = jnp.concatenate([(beta * jnp.exp(g))[..., None] * k, beta[..., None] * v], axis=-1)
    wu = lax.linalg.triangular_solve(a_mat + jnp.eye(L, dtype=a_mat.dtype), rhs,
                                     left_side=True, lower=True, unit_diagonal=True)
    w, u = wu[..., :dk], wu[..., dk:]
    attn = jnp.einsum('bhnid,bhnjd->bhnij', q, k) * decay
    q_dec = q * jnp.exp(g)[..., None]
    k_dec = k * jnp.exp(g[..., -1:] - g)[..., None]
    g_end = jnp.exp(g[..., -1])

    def step(s, xs):
        w_c, u_c, attn_c, qd_c, kd_c, ge_c = xs
        delta = u_c - jnp.einsum('bhlk,bhkv->bhlv', w_c, s)
        o_c = jnp.einsum('bhlk,bhkv->bhlv', qd_c, s) + jnp.einsum('bhij,bhjv->bhiv', attn_c, delta)
        s = ge_c[..., None, None] * s + jnp.einsum('bhlk,bhlv->bhkv', kd_c, delta)
        return s, o_c

    xs = tuple(jnp.moveaxis(a, 2, 0) for a in (w, u, attn, q_dec, k_dec, g_end))
    s_fin, o = lax.scan(step, s0, xs)
    o = jnp.transpose(o, (1, 0, 3, 2, 4)).reshape(bsz, t, nh, v.shape[-1])
    return o, s_fin


def _lin_combine(e1, e2):
    a1, b1 = e1
    a2, b2 = e2
    return a1 * a2, a2 * b1 + b2


def s5_scan(u, h0_re, h0_im, lam_re, lam_im, log_dt, b_re, b_im, c_re, c_im, d_skip):
    f32 = jnp.float32
    bsz, t, _ = u.shape
    uf = u.astype(f32).reshape(bsz, t, SSM_GROUPS, SSM_GROUP)
    lam = lax.complex(lam_re.astype(f32), lam_im.astype(f32))
    lam_bar = jnp.exp(lam * jnp.exp(log_dt.astype(f32))[:, None])
    b_bar = ((lam_bar - 1.0) / lam)[..., None] * lax.complex(b_re.astype(f32), b_im.astype(f32))
    bu = jnp.einsum('gpc,btgc->btgp', b_bar, uf.astype(jnp.complex64))
    h0 = lax.complex(h0_re.astype(f32), h0_im.astype(f32))
    bu = bu.at[:, 0].add(lam_bar * h0)
    a = jnp.broadcast_to(lam_bar, bu.shape)
    _, h = lax.associative_scan(_lin_combine, (a, bu), axis=1)
    c = lax.complex(c_re.astype(f32), c_im.astype(f32))
    y = jnp.einsum('gcp,btgp->btgc', c, h).real + d_skip.astype(f32) * uf
    h_last = h[:, -1]
    return y.reshape(bsz, t, D_B), h_last.real, h_last.imag


def mlstm_chunkwise(q, k, v, i_pre, log_f, c0, n0, m0):
    bsz, t, nh, dh = q.shape
    L = min(CHUNK, t)
    nc = t // L
    q, k, v = _blk(q, nc, L), _blk(k, nc, L), _blk(v, nc, L)
    i_pre = _blk(i_pre, nc, L)
    b = jnp.cumsum(_blk(log_f, nc, L), axis=-1)
    incl = jnp.tril(jnp.ones((L, L), dtype=bool))
    logw = jnp.where(incl, b[..., :, None] - b[..., None, :] + i_pre[..., None, :], -jnp.inf)
    logw_max = jnp.max(logw, axis=-1)
    qk = jnp.einsum('bhnid,bhnjd->bhnij', q, k)
    logw_end = b[..., -1:] - b + i_pre

    def step(carry, xs):
        c, n, m = carry
        q_c, k_c, v_c, qk_c, lw_c, lwm_c, b_c, le_c = xs
        m_t = jnp.maximum(b_c + m[..., None], lwm_c)
        w_state = jnp.exp(b_c + m[..., None] - m_t)
        w_intra = jnp.exp(lw_c - m_t[..., None]) * qk_c
        num = w_state[..., None] * jnp.einsum('bhld,bhde->bhle', q_c, c) + jnp.einsum('bhls,bhse->bhle', w_intra, v_c)
        den = w_state * jnp.einsum('bhld,bhd->bhl', q_c, n) + jnp.sum(w_intra, axis=-1)
        h_c = num / jnp.maximum(jnp.abs(den), jnp.exp(-m_t))[..., None]
        m_new = m_t[..., -1]
        w_keep = jnp.exp(b_c[..., -1] + m - m_new)
        w_end = jnp.exp(le_c - m_new[..., None])
        c = w_keep[..., None, None] * c + jnp.einsum('bhl,bhld,bhle->bhde', w_end, k_c, v_c)
        n = w_keep[..., None] * n + jnp.einsum('bhl,bhld->bhd', w_end, k_c)
        return (c, n, m_new), h_c

    xs = tuple(jnp.moveaxis(a, 2, 0) for a in (q, k, v, qk, logw, logw_max, b, logw_end))
    (c_fin, n_fin, m_fin), h = lax.scan(step, (c0, n0, m0), xs)
    h = jnp.transpose(h, (1, 0, 3, 2, 4)).reshape(bsz, t, nh, dh)
    return h, c_fin, n_fin, m_fin


def mixer_layer(x, conv_buf, dn_s0, ssm_h0_re, ssm_h0_im, ml_c0, ml_n0, ml_m0,
                norm_pre, norm_post, w_in, dn_conv_w, dn_A_log, dn_dt_bias, dn_norm,
                ssm_lam_re, ssm_lam_im, ssm_log_dt, ssm_B_re, ssm_B_im, ssm_C_re, ssm_C_im,
                ssm_D, ssm_glu_w, ssm_glu_b, ml_i_bias, ml_f_bias, ml_norm, w_out):
    f32 = jnp.float32
    bsz, t, _ = x.shape
    h = rms_norm(x, norm_pre)
    proj = jnp.einsum('btd,dp->btp', h, w_in.astype(h.dtype))
    (a_qkv, a_z, a_beta, a_alpha, b_u, b_z,
     c_q, c_k, c_v, c_o, c_z, c_i, c_f) = jnp.split(proj, PROJ_SPLITS, axis=-1)

    qkv, conv_new = causal_conv(a_qkv, conv_buf, dn_conv_w)
    qkv = jax.nn.silu(qkv.astype(f32))
    aq, ak, av = jnp.split(qkv, 3, axis=-1)
    aq = l2norm(aq.reshape(bsz, t, DN_HEADS, DN_DK)) * (DN_DK ** -0.5)
    ak = l2norm(ak.reshape(bsz, t, DN_HEADS, DN_DK))
    av = av.reshape(bsz, t, DN_HEADS, DN_DV)
    beta = jax.nn.sigmoid(a_beta.astype(f32))
    log_alpha = -jnp.exp(dn_A_log.astype(f32)) * jax.nn.softplus(a_alpha.astype(f32) + dn_dt_bias.astype(f32))
    o_a, dn_s = gated_delta_rule(aq, ak, av, beta, log_alpha, dn_s0.astype(f32))
    o_a = rms_norm(o_a, dn_norm).reshape(bsz, t, D_A) * jax.nn.silu(a_z.astype(f32))

    y_b, h_re, h_im = s5_scan(b_u, ssm_h0_re, ssm_h0_im, ssm_lam_re, ssm_lam_im, ssm_log_dt,
                              ssm_B_re, ssm_B_im, ssm_C_re, ssm_C_im, ssm_D)
    y_b = jax.nn.gelu(y_b)
    y_b = y_b * jax.nn.sigmoid(y_b @ ssm_glu_w.astype(f32) + ssm_glu_b.astype(f32))
    o_b = y_b * jax.nn.silu(b_z.astype(f32))

    mq = c_q.astype(f32).reshape(bsz, t, ML_HEADS, ML_DH)
    mk = c_k.astype(f32).reshape(bsz, t, ML_HEADS, ML_DH) * (ML_DH ** -0.5)
    mv = c_v.astype(f32).reshape(bsz, t, ML_HEADS, ML_DH)
    i_pre = c_i.astype(f32) + ml_i_bias.astype(f32)
    log_f = jax.nn.log_sigmoid(c_f.astype(f32) + ml_f_bias.astype(f32))
    h_c, ml_c, ml_n, ml_m = mlstm_chunkwise(mq, mk, mv, i_pre, log_f, ml_c0.astype(f32),
                                            ml_n0.astype(f32), ml_m0.astype(f32))
    h_c = jax.nn.sigmoid(c_o.astype(f32)).reshape(bsz, t, ML_HEADS, ML_DH) * h_c
    o_c = rms_norm(h_c, ml_norm).reshape(bsz, t, D_C) * jax.nn.silu(c_z.astype(f32))

    mix = jnp.concatenate([o_a, o_b, o_c], axis=-1).astype(x.dtype)
    out = jnp.einsum('btm,md->btd', mix, w_out.astype(x.dtype))
    x = x + rms_norm(out, norm_post)
    return x, (conv_new, dn_s, h_re, h_im, ml_c, ml_n, ml_m)


def setup_inputs(seed: int = 0) -> dict:
    key = jax.random.key(seed)
    ks = jax.random.split(key, 32)
    f32 = jnp.float32

    def nrm(i, shape, scale):
        return scale * jax.random.normal(ks[i], shape, f32)

    def unif(i, shape, lo, hi):
        return jax.random.uniform(ks[i], shape, f32, lo, hi)

    dt_dn = jnp.exp(unif(10, (DEPTH, DN_HEADS), math.log(1e-3), math.log(1e-1)))
    return {
        'x_prompt': nrm(0, (BATCH, SEQ, D_MODEL), 1.0),
        'x_sample': nrm(1, (DEC_BATCH, DEC_SEQ, D_MODEL), 1.0),
        'state_dn_conv': nrm(2, (DEPTH, DEC_BATCH, CONV_W - 1, 3 * D_A), 1.0),
        'state_dn_S': nrm(3, (DEPTH, DEC_BATCH, DN_HEADS, DN_DK, DN_DV), 0.1),
        'state_ssm_re': nrm(4, (DEPTH, DEC_BATCH, SSM_GROUPS, SSM_STATE), 0.1),
        'state_ssm_im': nrm(5, (DEPTH, DEC_BATCH, SSM_GROUPS, SSM_STATE), 0.1),
        'state_ml_C': nrm(6, (DEPTH, DEC_BATCH, ML_HEADS, ML_DH, ML_DH), 0.05),
        'state_ml_n': nrm(7, (DEPTH, DEC_BATCH, ML_HEADS, ML_DH), 0.1),
        'state_ml_m': nrm(8, (DEPTH, DEC_BATCH, ML_HEADS), 1.0),
        'norm_pre': 1.0 + nrm(9, (DEPTH, D_MODEL), 0.02),
        'norm_post': 1.0 + nrm(11, (DEPTH, D_MODEL), 0.02),
        'w_in': nrm(12, (DEPTH, D_MODEL, D_PROJ), D_MODEL ** -0.5),
        'dn_conv_w': nrm(13, (DEPTH, CONV_W, 3 * D_A), CONV_W ** -0.5),
        'dn_A_log': jnp.log(unif(14, (DEPTH, DN_HEADS), 1.0, 16.0)),
        'dn_dt_bias': dt_dn + jnp.log(-jnp.expm1(-dt_dn)),
        'dn_norm': 1.0 + nrm(15, (DEPTH, DN_DV), 0.02),
        'ssm_lam_re': -0.5 + nrm(16, (DEPTH, SSM_GROUPS, SSM_STATE), 0.01),
        'ssm_lam_im': math.pi * jnp.arange(SSM_STATE, dtype=f32) + nrm(17, (DEPTH, SSM_GROUPS, SSM_STATE), 0.01),
        'ssm_log_dt': unif(18, (DEPTH, SSM_GROUPS), math.log(1e-3), math.log(1e-1)),
        'ssm_B_re': nrm(19, (DEPTH, SSM_GROUPS, SSM_STATE, SSM_GROUP), (2 * SSM_GROUP) ** -0.5),
        'ssm_B_im': nrm(20, (DEPTH, SSM_GROUPS, SSM_STATE, SSM_GROUP), (2 * SSM_GROUP) ** -0.5),
        'ssm_C_re': nrm(21, (DEPTH, SSM_GROUPS, SSM_GROUP, SSM_STATE), (2 * SSM_STATE) ** -0.5),
        'ssm_C_im': nrm(22, (DEPTH, SSM_GROUPS, SSM_GROUP, SSM_STATE), (2 * SSM_STATE) ** -0.5),
        'ssm_D': nrm(23, (DEPTH, SSM_GROUPS, SSM_GROUP), 1.0),
        'ssm_glu_w': nrm(24, (DEPTH, D_B, D_B), D_B ** -0.5),
        'ssm_glu_b': nrm(25, (DEPTH, D_B), 0.01),
        'ml_i_bias': nrm(26, (DEPTH, ML_HEADS), 0.1),
        'ml_f_bias': jnp.linspace(3.0, 6.0, ML_HEADS, dtype=f32) + nrm(27, (DEPTH, ML_HEADS), 0.1),
        'ml_norm': 1.0 + nrm(28, (DEPTH, ML_DH), 0.02),
        'w_out': nrm(29, (DEPTH, D_MIX, D_MODEL), D_MIX ** -0.5),
    }


def reference(x_prompt, x_sample, state_dn_conv, state_dn_S, state_ssm_re, state_ssm_im,
              state_ml_C, state_ml_n, state_ml_m, norm_pre, norm_post, w_in, dn_conv_w,
              dn_A_log, dn_dt_bias, dn_norm, ssm_lam_re, ssm_lam_im, ssm_log_dt,
              ssm_B_re, ssm_B_im, ssm_C_re, ssm_C_im, ssm_D, ssm_glu_w, ssm_glu_b,
              ml_i_bias, ml_f_bias, ml_norm, w_out):
    f32 = jnp.float32
    bp = x_prompt.shape[0]
    zero_state = (jnp.zeros((bp, CONV_W - 1, 3 * D_A), x_prompt.dtype),
                  jnp.zeros((bp, DN_HEADS, DN_DK, DN_DV), f32),
                  jnp.zeros((bp, SSM_GROUPS, SSM_STATE), f32),
                  jnp.zeros((bp, SSM_GROUPS, SSM_STATE), f32),
                  jnp.zeros((bp, ML_HEADS, ML_DH, ML_DH), f32),
                  jnp.zeros((bp, ML_HEADS, ML_DH), f32),
                  jnp.zeros((bp, ML_HEADS), f32))
    xp, xs = x_prompt, x_sample
    new_p, new_s = [], []
    for l in range(DEPTH):
        wl = (norm_pre[l], norm_post[l], w_in[l], dn_conv_w[l], dn_A_log[l], dn_dt_bias[l], dn_norm[l],
              ssm_lam_re[l], ssm_lam_im[l], ssm_log_dt[l], ssm_B_re[l], ssm_B_im[l], ssm_C_re[l],
              ssm_C_im[l], ssm_D[l], ssm_glu_w[l], ssm_glu_b[l], ml_i_bias[l], ml_f_bias[l],
              ml_norm[l], w_out[l])
        xp, sp = mixer_layer(xp, *zero_state, *wl)
        carried = (state_dn_conv[l], state_dn_S[l], state_ssm_re[l], state_ssm_im[l],
                   state_ml_C[l], state_ml_n[l], state_ml_m[l])
        xs, ss = mixer_layer(xs, *carried, *wl)
        new_p.append(sp)
        new_s.append(ss)

    def stack(states, i, dtype):
        return jnp.stack([s[i] for s in states], axis=0).astype(dtype)

    p = [stack(new_p, i, x_prompt.dtype) for i in range(7)]
    s = [stack(new_s, i, x_sample.dtype) for i in range(7)]
    return (xp, xs, p[0], p[1], p[2], p[3], p[4], p[5], p[6], s[0], s[1], s[2], s[3], s[4], s[5], s[6])
```

```python
import functools
import math

import jax
import jax.numpy as jnp
from jax import lax
from jax.experimental import pallas as pl
from jax.experimental.pallas import tpu as pltpu

F32 = jnp.float32
BF16 = jnp.bfloat16
HIGHEST = lax.Precision.HIGHEST

D_MODEL = 1024
DEPTH = 2
CHUNK = 64
D_MIX = 2 * D_MODEL
D_A = D_MIX // 2
DN_DK = 128
DN_DV = 128
DN_HEADS = D_A // DN_DV
CONV_W = 4
D_B = D_MIX // 4
SSM_GROUP = 16
SSM_GROUPS = D_B // SSM_GROUP
SSM_STATE = 64
D_C = D_MIX // 4
ML_DH = 128
ML_HEADS = D_C // ML_DH
EPS = 1e-6
N_SSM = SSM_GROUPS * SSM_STATE

LANES = 128
SUBLANES = 8

COL_SMALL = 3 * D_A + D_A + 2 * D_B + 5 * D_C
DP = 8192
SM_BETA = 0
SM_ALPHA = DN_HEADS
SM_I = 2 * DN_HEADS
SM_F = 2 * DN_HEADS + ML_HEADS

VMEM_LIMIT = 48 * 1024 * 1024


def _dot(a, b):
    return jnp.dot(a.astype(BF16), b.astype(BF16), preferred_element_type=F32)


def _dot_nt(a, b):
    return lax.dot_general(a.astype(BF16), b.astype(BF16), (((1,), (1,)), ((), ())),
                           preferred_element_type=F32)


def _dot_tn(a, b):
    return lax.dot_general(a.astype(BF16), b.astype(BF16), (((0,), (0,)), ((), ())),
                           preferred_element_type=F32)


def _dot_exact(a, b):
    return jnp.dot(a, b, precision=HIGHEST, preferred_element_type=F32)


def _transpose_exact(x):
    n = x.shape[1]
    eye = (lax.broadcasted_iota(jnp.int32, (n, n), 0) == lax.broadcasted_iota(jnp.int32, (n, n), 1)).astype(F32)
    return lax.dot_general(eye, x, (((1,), (1,)), ((), ())), precision=HIGHEST, preferred_element_type=F32)


def _sigmoid(x):
    return 1.0 / (1.0 + jnp.exp(-x))


def _silu(x):
    return x * _sigmoid(x)


def _softplus(x):
    return jnp.maximum(x, 0.0) + jnp.log(1.0 + jnp.exp(-jnp.abs(x)))


def _tri_masks(L):
    row = lax.broadcasted_iota(jnp.int32, (L, L), 0)
    col = lax.broadcasted_iota(jnp.int32, (L, L), 1)
    return row, col, row >= col, row > col


def _rms(x, g):
    return x * lax.rsqrt(jnp.mean(x * x, axis=-1, keepdims=True) + EPS) * g


def _proj_kernel(x_ref, g_ref, w_ref, o_ref, h_scr):
    @pl.when(pl.program_id(1) == 0)
    def _():
        h_scr[...] = _rms(x_ref[...], g_ref[...]).astype(BF16)

    o_ref[...] = jnp.dot(h_scr[...], w_ref[...], preferred_element_type=F32)


def _proj(x2, g_row, w_bf, tm, tn):
    n = x2.shape[0]
    return pl.pallas_call(
        _proj_kernel,
        grid=(n // tm, DP // tn),
        in_specs=[pl.BlockSpec((tm, D_MODEL), lambda i, j: (i, 0)),
                  pl.BlockSpec((1, D_MODEL), lambda i, j: (0, 0)),
                  pl.BlockSpec((D_MODEL, tn), lambda i, j: (0, j))],
        out_specs=pl.BlockSpec((tm, tn), lambda i, j: (i, j)),
        out_shape=jax.ShapeDtypeStruct((n, DP), F32),
        scratch_shapes=[pltpu.VMEM((tm, D_MODEL), BF16)],
        compiler_params=pltpu.CompilerParams(dimension_semantics=("parallel", "arbitrary"),
                                             vmem_limit_bytes=VMEM_LIMIT),
        name="proj",
    )(x2, g_row, w_bf)


def _unit_lower_inverse(a_strict, row, col, L):
    t = None
    s = 1
    while s < L:
        sh = s.bit_length() - 1
        m = ((row >> (sh + 1)) == (col >> (sh + 1))) & (((row >> sh) & 1) == 1) & (((col >> sh) & 1) == 0)
        a_off = jnp.where(m, a_strict, 0.0)
        if t is None:
            t = jnp.where(row == col, 1.0, 0.0) - a_off
        else:
            t = t - _dot(t, _dot(a_off, t))
        s *= 2
    return t


def _dn_kernel(q_ref, k_ref, v_ref, z_ref, sm_ref, conv0_ref, s0_ref, cw_ref, alog_ref, dtb_ref, nrm_ref,
               oa_ref, sout_ref, cout_ref, s_scr, xp_scr, *, L):
    n = pl.program_id(1)
    nc = pl.num_programs(1)

    @pl.when(n == 0)
    def _():
        s_scr[...] = s0_ref[0]
        xp_scr[0:SUBLANES, :] = conv0_ref[0]

    xp_scr[SUBLANES:SUBLANES + L, 0:D_A] = q_ref[0]
    xp_scr[SUBLANES:SUBLANES + L, D_A:2 * D_A] = k_ref[0]
    xp_scr[SUBLANES:SUBLANES + L, 2 * D_A:3 * D_A] = v_ref[0]

    row, col, incl, strict = _tri_masks(L)
    tri = incl.astype(F32)

    sm = sm_ref[0]
    beta_all = _sigmoid(sm)
    la_all = -jnp.exp(alog_ref[...]) * _softplus(sm + dtb_ref[...])
    g_all = _dot_exact(tri, la_all)
    g_t = _transpose_exact(g_all)

    def conv_silu(c0):
        base = SUBLANES - (CONV_W - 1)
        acc = None
        for j in range(CONV_W):
            term = xp_scr[base + j:base + j + L, c0:c0 + LANES] * cw_ref[j:j + 1, c0:c0 + LANES]
            acc = term if acc is None else acc + term
        return _silu(acc)

    for h in range(DN_HEADS):
        q = conv_silu(h * DN_DK)
        k = conv_silu(D_A + h * DN_DK)
        v = conv_silu(2 * D_A + h * DN_DV)
        q = q * lax.rsqrt(jnp.sum(q * q, axis=-1, keepdims=True) + EPS) * (DN_DK ** -0.5)
        k = k * lax.rsqrt(jnp.sum(k * k, axis=-1, keepdims=True) + EPS)
        beta = beta_all[:, SM_BETA + h:SM_BETA + h + 1]
        g = g_all[:, SM_ALPHA + h:SM_ALPHA + h + 1]
        g_row = g_t[SM_ALPHA + h:SM_ALPHA + h + 1, :]
        g_last = g[L - 1:L, :]
        eg = jnp.exp(g)
        decay = jnp.where(incl, jnp.exp(jnp.where(incl, g - g_row, 0.0)), 0.0)
        a_mat = jnp.where(strict, beta * _dot_nt(k, k) * decay, 0.0)
        t_inv = _unit_lower_inverse(a_mat, row, col, L)
        rhs = jnp.concatenate([k * (beta * eg), v * beta], axis=1)
        wu = _dot(t_inv, rhs)
        w = wu[:, :DN_DK]
        u = wu[:, DN_DK:]
        attn = _dot_nt(q, k) * decay
        s = s_scr[h]
        delta = u - _dot(w, s)
        o = _dot(q * eg, s) + _dot(attn, delta)
        s_scr[h] = jnp.exp(g_last) * s + _dot_tn(k * jnp.exp(g_last - g), delta)
        o = _rms(o, nrm_ref[...]) * _silu(z_ref[0, :, h * DN_DV:(h + 1) * DN_DV])
        oa_ref[0, :, h * DN_DV:(h + 1) * DN_DV] = o

    xp_scr[0:SUBLANES, :] = xp_scr[L:L + SUBLANES, :]

    @pl.when(n == nc - 1)
    def _():
        sout_ref[0] = s_scr[...]
        cout_ref[0] = xp_scr[0:SUBLANES, :]


def _dn(proj3, conv0p, s0, cw, alog_row, dtb_row, nrm_row, L):
    b, t, _ = proj3.shape
    nc = t // L
    col = lambda c: (lambda i, n: (i, n, c))
    fixed2 = lambda i, n: (0, 0)
    return pl.pallas_call(
        functools.partial(_dn_kernel, L=L),
        grid=(b, nc),
        in_specs=[pl.BlockSpec((1, L, D_A), col(0)),
                  pl.BlockSpec((1, L, D_A), col(1)),
                  pl.BlockSpec((1, L, D_A), col(2)),
                  pl.BlockSpec((1, L, D_A), col(3)),
                  pl.BlockSpec((1, L, LANES), col(COL_SMALL // LANES)),
                  pl.BlockSpec((1, SUBLANES, 3 * D_A), lambda i, n: (i, 0, 0)),
                  pl.BlockSpec((1, DN_HEADS, DN_DK, DN_DV), lambda i, n: (i, 0, 0, 0)),
                  pl.BlockSpec((CONV_W, 3 * D_A), fixed2),
                  pl.BlockSpec((1, LANES), fixed2),
                  pl.BlockSpec((1, LANES), fixed2),
                  pl.BlockSpec((1, DN_DV), fixed2)],
        out_specs=[pl.BlockSpec((1, L, D_A), lambda i, n: (i, n, 0)),
                   pl.BlockSpec((1, DN_HEADS, DN_DK, DN_DV), lambda i, n: (i, 0, 0, 0)),
                   pl.BlockSpec((1, SUBLANES, 3 * D_A), lambda i, n: (i, 0, 0))],
        out_shape=[jax.ShapeDtypeStruct((b, t, D_A), F32),
                   jax.ShapeDtypeStruct((b, DN_HEADS, DN_DK, DN_DV), F32),
                   jax.ShapeDtypeStruct((b, SUBLANES, 3 * D_A), F32)],
        scratch_shapes=[pltpu.VMEM((DN_HEADS, DN_DK, DN_DV), F32),
                        pltpu.VMEM((L + SUBLANES, 3 * D_A), F32)],
        compiler_params=pltpu.CompilerParams(dimension_semantics=("parallel", "arbitrary"),
                                             vmem_limit_bytes=VMEM_LIMIT),
        name="dn",
    )(proj3, proj3, proj3, proj3, proj3, conv0p, s0, cw, alog_row, dtb_row, nrm_row)


def _s5prep_kernel(a_ref, th_ref, lr_ref, li_ref, bre_ref, bim_ref, a16_ref, th16_ref,
                   pnr_ref, pni_ref, ppr_ref, ppi_ref, l1r_ref, l1i_ref, bbr_ref, bbi_ref, *, L):
    tcol = lax.broadcasted_iota(jnp.int32, (L, 1), 0).astype(F32)
    a = a_ref[...]
    th = th_ref[...]
    ppr_ref[...] = jnp.exp(a * tcol) * jnp.cos(th * tcol)
    ppi_ref[...] = jnp.exp(a * tcol) * jnp.sin(th * tcol)
    pnr_ref[...] = jnp.exp(-a * tcol) * jnp.cos(th * tcol)
    pni_ref[...] = -jnp.exp(-a * tcol) * jnp.sin(th * tcol)
    l1r_ref[...] = jnp.exp(a) * jnp.cos(th)
    l1i_ref[...] = jnp.exp(a) * jnp.sin(th)
    a16 = a16_ref[...]
    th16 = th16_ref[...]
    lr = lr_ref[...]
    li = li_ref[...]
    nr = jnp.exp(a16) * jnp.cos(th16) - 1.0
    ni = jnp.exp(a16) * jnp.sin(th16)
    den = lr * lr + li * li
    cr = (nr * lr + ni * li) / den
    ci = (ni * lr - nr * li) / den
    bre = bre_ref[...]
    bim = bim_ref[...]
    bbr_ref[...] = cr * bre - ci * bim
    bbi_ref[...] = cr * bim + ci * bre


def _s5prep(lam_re, lam_im, log_dt, b_re, b_im, L):
    dt = jnp.exp(log_dt)[:, None]
    a_row = (lam_re * dt).reshape(1, N_SSM)
    th_row = (lam_im * dt).reshape(1, N_SSM)
    rep = lambda x: jnp.broadcast_to(x.reshape(N_SSM, 1), (N_SSM, SSM_GROUP))
    outs = pl.pallas_call(
        functools.partial(_s5prep_kernel, L=L),
        out_shape=[jax.ShapeDtypeStruct((L, N_SSM), F32)] * 4
        + [jax.ShapeDtypeStruct((1, N_SSM), F32)] * 2
        + [jax.ShapeDtypeStruct((N_SSM, SSM_GROUP), F32)] * 2,
        name="s5prep",
    )(a_row, th_row, rep(lam_re), rep(lam_im), b_re.reshape(N_SSM, SSM_GROUP), b_im.reshape(N_SSM, SSM_GROUP),
      rep(lam_re * dt), rep(lam_im * dt))
    return outs


def _s5_kernel(u_ref, z_ref, h0r_ref, h0i_ref, pnr_ref, pni_ref, ppr_ref, ppi_ref, l1r_ref, l1i_ref,
               bre_ref, bim_ref, cre_ref, cim_ref, d_ref, gw_ref, gb_ref,
               ob_ref, hr_out, hi_out, hr_scr, hi_scr, *, L):
    n = pl.program_id(1)
    nc = pl.num_programs(1)

    @pl.when(n == 0)
    def _():
        hr_scr[...] = h0r_ref[0]
        hi_scr[...] = h0i_ref[0]

    _, _, incl, _ = _tri_masks(L)
    tri = incl.astype(BF16)
    u = u_ref[0]
    ub = u.astype(BF16)
    bur = jnp.dot(ub, bre_ref[...], preferred_element_type=F32)
    bui = jnp.dot(ub, bim_ref[...], preferred_element_type=F32)
    pnr = pnr_ref[...]
    pni = pni_ref[...]
    vr = pnr * bur - pni * bui
    vi = pnr * bui + pni * bur
    hr0 = hr_scr[...]
    hi0 = hi_scr[...]
    l1r = l1r_ref[...]
    l1i = l1i_ref[...]
    cr = _dot(tri, vr) + (l1r * hr0 - l1i * hi0)
    ci = _dot(tri, vi) + (l1r * hi0 + l1i * hr0)
    ppr = ppr_ref[...]
    ppi = ppi_ref[...]
    hr = ppr * cr - ppi * ci
    hi = ppr * ci + ppi * cr
    hr_scr[...] = hr[L - 1:L, :]
    hi_scr[...] = hi[L - 1:L, :]
    y = _dot(hr, cre_ref[...]) - _dot(hi, cim_ref[...]) + d_ref[...] * u
    c0 = math.sqrt(2.0 / math.pi)
    y = 0.5 * y * (1.0 + jnp.tanh(c0 * (y + 0.044715 * (y * y * y))))
    y = y * _sigmoid(_dot(y, gw_ref[...]) + gb_ref[...])
    ob_ref[0] = y * _silu(z_ref[0])

    @pl.when(n == nc - 1)
    def _():
        hr_out[0] = hr_scr[...]
        hi_out[0] = hi_scr[...]


def _s5(proj3, h0r, h0i, tabs, bbd_re, bbd_im, cbd_re, cbd_im, d_row, gw, gb_row, L):
    b, t, _ = proj3.shape
    nc = t // L
    pnr, pni, ppr, ppi, l1r, l1i = tabs
    fixed2 = lambda i, n: (0, 0)
    full = lambda a: pl.BlockSpec(a.shape, fixed2)
    state = pl.BlockSpec((1, 1, N_SSM), lambda i, n: (i, 0, 0))
    u_col = (3 * D_A + D_A) // D_B
    return pl.pallas_call(
        functools.partial(_s5_kernel, L=L),
        grid=(b, nc),
        in_specs=[pl.BlockSpec((1, L, D_B), lambda i, n: (i, n, u_col)),
                  pl.BlockSpec((1, L, D_B), lambda i, n: (i, n, u_col + 1)),
                  state, state,
                  full(pnr), full(pni), full(ppr), full(ppi), full(l1r), full(l1i),
                  full(bbd_re), full(bbd_im), full(cbd_re), full(cbd_im), full(d_row), full(gw), full(gb_row)],
        out_specs=[pl.BlockSpec((1, L, D_B), lambda i, n: (i, n, 0)), state, state],
        out_shape=[jax.ShapeDtypeStruct((b, t, D_B), F32),
                   jax.ShapeDtypeStruct((b, 1, N_SSM), F32),
                   jax.ShapeDtypeStruct((b, 1, N_SSM), F32)],
        scratch_shapes=[pltpu.VMEM((1, N_SSM), F32), pltpu.VMEM((1, N_SSM), F32)],
        compiler_params=pltpu.CompilerParams(dimension_semantics=("parallel", "arbitrary"),
                                             vmem_limit_bytes=VMEM_LIMIT),
        name="s5",
    )(proj3, proj3, h0r, h0i, pnr, pni, ppr, ppi, l1r, l1i, bbd_re, bbd_im, cbd_re, cbd_im, d_row, gw, gb_row)


def _ml_kernel(q_ref, k_ref, v_ref, o_ref, z_ref, sm_ref, c0_ref, n0_ref, m0_ref, ib_ref, fb_ref, nrm_ref,
               oc_ref, cout_ref, nout_ref, mout_ref, c_scr, n_scr, m_scr, *, L):
    n = pl.program_id(1)
    nc = pl.num_programs(1)

    @pl.when(n == 0)
    def _():
        c_scr[...] = c0_ref[0]
        n_scr[...] = n0_ref[0]
        m_scr[...] = m0_ref[0]

    row, col, incl, _ = _tri_masks(L)
    tri = incl.astype(F32)
    sm = sm_ref[0]
    ipre_all = sm + ib_ref[...]
    xf = sm + fb_ref[...]
    logf_all = jnp.minimum(xf, 0.0) - jnp.log(1.0 + jnp.exp(-jnp.abs(xf)))
    b_all = _dot_exact(tri, logf_all)
    b_t = _transpose_exact(b_all)
    i_t = _transpose_exact(ipre_all)
    lane = lax.broadcasted_iota(jnp.int32, (1, LANES), 1)
    m_all = m_scr[...]
    m_next = m_all

    for h in range(ML_HEADS):
        sl = slice(h * ML_DH, (h + 1) * ML_DH)
        q = q_ref[0, :, sl]
        k = k_ref[0, :, sl] * (ML_DH ** -0.5)
        v = v_ref[0, :, sl]
        b = b_all[:, SM_F + h:SM_F + h + 1]
        b_row = b_t[SM_F + h:SM_F + h + 1, :]
        i_col = ipre_all[:, SM_I + h:SM_I + h + 1]
        i_row = i_t[SM_I + h:SM_I + h + 1, :]
        m_prev = m_all[:, h:h + 1]
        logw = jnp.where(incl, b - b_row + i_row, -jnp.inf)
        lwm = jnp.max(logw, axis=-1, keepdims=True)
        m_t = jnp.maximum(b + m_prev, lwm)
        w_state = jnp.exp(b + m_prev - m_t)
        w_intra = jnp.exp(logw - m_t) * _dot_nt(q, k)
        c = c_scr[h]
        nvec = n_scr[h:h + 1, :]
        num = w_state * _dot(q, c) + _dot(w_intra, v)
        den = w_state * jnp.sum(q * nvec, axis=-1, keepdims=True) + jnp.sum(w_intra, axis=-1, keepdims=True)
        hh = num / jnp.maximum(jnp.abs(den), jnp.exp(-m_t))
        m_new = m_t[L - 1:L, :]
        b_last = b[L - 1:L, :]
        w_keep = jnp.exp(b_last + m_prev - m_new)
        w_end = jnp.exp(b_last - b + i_col - m_new)
        kw = k * w_end
        c_scr[h] = w_keep * c + _dot_tn(kw, v)
        n_scr[h:h + 1, :] = w_keep * nvec + jnp.sum(kw, axis=0, keepdims=True)
        m_next = jnp.where(lane == h, m_new, m_next)
        hh = _sigmoid(o_ref[0, :, sl]) * hh
        oc_ref[0, :, sl] = _rms(hh, nrm_ref[...]) * _silu(z_ref[0, :, sl])

    m_scr[...] = m_next

    @pl.when(n == nc - 1)
    def _():
        cout_ref[0] = c_scr[...]
        nout_ref[0] = n_scr[...]
        mout_ref[0] = m_scr[...]


def _ml(proj3, c0, n0, m0p, ib_row, fb_row, nrm_row, L):
    b, t, _ = proj3.shape
    nc = t // L
    c_col = (3 * D_A + D_A + 2 * D_B) // D_C
    col = lambda c: (lambda i, n: (i, n, c))
    fixed2 = lambda i, n: (0, 0)
    cspec = pl.BlockSpec((1, ML_HEADS, ML_DH, ML_DH), lambda i, n: (i, 0, 0, 0))
    nspec = pl.BlockSpec((1, ML_HEADS, ML_DH), lambda i, n: (i, 0, 0))
    mspec = pl.BlockSpec((1, 1, LANES), lambda i, n: (i, 0, 0))
    return pl.pallas_call(
        functools.partial(_ml_kernel, L=L),
        grid=(b, nc),
        in_specs=[pl.BlockSpec((1, L, D_C), col(c_col + j)) for j in range(5)]
        + [pl.BlockSpec((1, L, LANES), col(COL_SMALL // LANES)),
           cspec, nspec, mspec,
           pl.BlockSpec((1, LANES), fixed2), pl.BlockSpec((1, LANES), fixed2), pl.BlockSpec((1, ML_DH), fixed2)],
        out_specs=[pl.BlockSpec((1, L, D_C), lambda i, n: (i, n, 0)), cspec, nspec, mspec],
        out_shape=[jax.ShapeDtypeStruct((b, t, D_C), F32),
                   jax.ShapeDtypeStruct((b, ML_HEADS, ML_DH, ML_DH), F32),
                   jax.ShapeDtypeStruct((b, ML_HEADS, ML_DH), F32),
                   jax.ShapeDtypeStruct((b, 1, LANES), F32)],
        scratch_shapes=[pltpu.VMEM((ML_HEADS, ML_DH, ML_DH), F32),
                        pltpu.VMEM((ML_HEADS, ML_DH), F32),
                        pltpu.VMEM((1, LANES), F32)],
        compiler_params=pltpu.CompilerParams(dimension_semantics=("parallel", "arbitrary"),
                                             vmem_limit_bytes=VMEM_LIMIT),
        name="ml",
    )(proj3, proj3, proj3, proj3, proj3, proj3, c0, n0, m0p, ib_row, fb_row, nrm_row)


def _out_kernel(oa_ref, ob_ref, oc_ref, wa_ref, wb_ref, wc_ref, x_ref, g_ref, y_ref):
    acc = jnp.dot(oa_ref[...].astype(BF16), wa_ref[...], preferred_element_type=F32)
    acc = acc + jnp.dot(ob_ref[...].astype(BF16), wb_ref[...], preferred_element_type=F32)
    acc = acc + jnp.dot(oc_ref[...].astype(BF16), wc_ref[...], preferred_element_type=F32)
    y_ref[...] = x_ref[...] + _rms(acc, g_ref[...])


def _outproj(oa, ob, oc, wa, wb, wc, x2, g_row, tm):
    n = x2.shape[0]
    rows = lambda w: pl.BlockSpec((tm, w), lambda i: (i, 0))
    full = lambda a: pl.BlockSpec(a.shape, lambda i: (0, 0))
    return pl.pallas_call(
        _out_kernel,
        grid=(n // tm,),
        in_specs=[rows(D_A), rows(D_B), rows(D_C), full(wa), full(wb), full(wc), rows(D_MODEL), full(g_row)],
        out_specs=rows(D_MODEL),
        out_shape=jax.ShapeDtypeStruct((n, D_MODEL), F32),
        compiler_params=pltpu.CompilerParams(dimension_semantics=("parallel",),
                                             vmem_limit_bytes=VMEM_LIMIT),
        name="outproj",
    )(oa, ob, oc, wa, wb, wc, x2, g_row)


def _pad_lanes(v, offset):
    return jnp.zeros((1, LANES), F32).at[0, offset:offset + v.shape[0]].set(v.astype(F32))


def _prep_layer_weights(w):
    (norm_pre, norm_post, w_in, dn_conv_w, dn_A_log, dn_dt_bias, dn_norm, ssm_lam_re, ssm_lam_im, ssm_log_dt,
     ssm_B_re, ssm_B_im, ssm_C_re, ssm_C_im, ssm_D, ssm_glu_w, ssm_glu_b, ml_i_bias, ml_f_bias, ml_norm, w_out) = w
    o = 0
    seg = {}
    for name, size in (("qkv", 3 * D_A), ("az", D_A), ("beta", DN_HEADS), ("alpha", DN_HEADS), ("bu", D_B),
                       ("bz", D_B), ("cq", D_C), ("ck", D_C), ("cv", D_C), ("co", D_C), ("cz", D_C),
                       ("ci", ML_HEADS), ("cf", ML_HEADS)):
        seg[name] = w_in[:, o:o + size]
        o += size
    n_small = 2 * DN_HEADS + 2 * ML_HEADS
    w_re = jnp.concatenate(
        [seg[k] for k in ("qkv", "az", "bu", "bz", "cq", "ck", "cv", "co", "cz", "beta", "alpha", "ci", "cf")]
        + [jnp.zeros((D_MODEL, DP - COL_SMALL - n_small), w_in.dtype)], axis=1).astype(BF16)
    eye_g = jnp.eye(SSM_GROUPS, dtype=F32)
    cbd = lambda c: jnp.einsum("gcp,gh->gphc", c.astype(F32), eye_g).reshape(N_SSM, D_B).astype(BF16)
    return dict(
        norm_pre=norm_pre.reshape(1, D_MODEL).astype(F32), norm_post=norm_post.reshape(1, D_MODEL).astype(F32),
        w_in=w_re, conv_w=dn_conv_w.astype(F32),
        alog_row=_pad_lanes(dn_A_log, SM_ALPHA), dtb_row=_pad_lanes(dn_dt_bias, SM_ALPHA),
        dn_norm=dn_norm.reshape(1, DN_DV).astype(F32),
        lam_re=ssm_lam_re.astype(F32), lam_im=ssm_lam_im.astype(F32), log_dt=ssm_log_dt.astype(F32),
        b_re=ssm_B_re.astype(F32), b_im=ssm_B_im.astype(F32), cbd_re=cbd(ssm_C_re), cbd_im=cbd(ssm_C_im),
        d_row=ssm_D.reshape(1, D_B).astype(F32), glu_w=ssm_glu_w.astype(BF16),
        glu_b=ssm_glu_b.reshape(1, D_B).astype(F32),
        ib_row=_pad_lanes(ml_i_bias, SM_I), fb_row=_pad_lanes(ml_f_bias, SM_F),
        ml_norm=ml_norm.reshape(1, ML_DH).astype(F32),
        wo_a=w_out[:D_A].astype(BF16), wo_b=w_out[D_A:D_A + D_B].astype(BF16), wo_c=w_out[D_A + D_B:].astype(BF16))


def _s5_tables(p, L):
    pnr, pni, ppr, ppi, l1r, l1i, bbr, bbi = _s5prep(p["lam_re"], p["lam_im"], p["log_dt"], p["b_re"], p["b_im"], L)
    eye_g = jnp.eye(SSM_GROUPS, dtype=F32)
    bbd = lambda bb: jnp.einsum("gpc,gh->gchp", bb.reshape(SSM_GROUPS, SSM_STATE, SSM_GROUP),
                                eye_g).reshape(D_B, N_SSM).astype(BF16)
    return (pnr, pni, ppr, ppi, l1r, l1i), bbd(bbr), bbd(bbi)


def _layer(x, state, p, s5tab, L, tm):
    conv0, s0, h0r, h0i, c0, n0, m0 = state
    b, t, _ = x.shape
    x2 = x.reshape(b * t, D_MODEL)
    proj3 = _proj(x2, p["norm_pre"], p["w_in"], tm, 512).reshape(b, t, DP)

    conv0p = jnp.pad(conv0.astype(F32), ((0, 0), (SUBLANES - (CONV_W - 1), 0), (0, 0)))
    oa, s_new, conv_new = _dn(proj3, conv0p, s0.astype(F32), p["conv_w"], p["alog_row"], p["dtb_row"],
                              p["dn_norm"], L)
    tabs, bbd_re, bbd_im = s5tab
    ob, hr_new, hi_new = _s5(proj3, h0r.reshape(b, 1, N_SSM).astype(F32), h0i.reshape(b, 1, N_SSM).astype(F32),
                             tabs, bbd_re, bbd_im, p["cbd_re"], p["cbd_im"], p["d_row"], p["glu_w"], p["glu_b"], L)
    m0p = jnp.pad(m0.astype(F32), ((0, 0), (0, LANES - ML_HEADS))).reshape(b, 1, LANES)
    oc, c_new, n_new, m_new = _ml(proj3, c0.astype(F32), n0.astype(F32), m0p, p["ib_row"], p["fb_row"],
                                  p["ml_norm"], L)
    y2 = _outproj(oa.reshape(b * t, D_A), ob.reshape(b * t, D_B), oc.reshape(b * t, D_C),
                  p["wo_a"], p["wo_b"], p["wo_c"], x2, p["norm_post"], tm)
    new_state = (conv_new[:, SUBLANES - (CONV_W - 1):, :], s_new,
                 hr_new.reshape(b, SSM_GROUPS, SSM_STATE), hi_new.reshape(b, SSM_GROUPS, SSM_STATE),
                 c_new, n_new, m_new[:, 0, :ML_HEADS])
    return y2.reshape(b, t, D_MODEL), new_state


def kernel(x_prompt, x_sample, state_dn_conv, state_dn_S, state_ssm_re, state_ssm_im, state_ml_C, state_ml_n, state_ml_m, norm_pre, norm_post, w_in, dn_conv_w, dn_A_log, dn_dt_bias, dn_norm, ssm_lam_re, ssm_lam_im, ssm_log_dt, ssm_B_re, ssm_B_im, ssm_C_re, ssm_C_im, ssm_D, ssm_glu_w, ssm_glu_b, ml_i_bias, ml_f_bias, ml_norm, w_out):
    bp, tp, _ = x_prompt.shape
    bs, ts, _ = x_sample.shape
    lp = min(CHUNK, tp)
    ls = min(CHUNK, ts)
    zero_state = (jnp.zeros((bp, CONV_W - 1, 3 * D_A), F32),
                  jnp.zeros((bp, DN_HEADS, DN_DK, DN_DV), F32),
                  jnp.zeros((bp, SSM_GROUPS, SSM_STATE), F32),
                  jnp.zeros((bp, SSM_GROUPS, SSM_STATE), F32),
                  jnp.zeros((bp, ML_HEADS, ML_DH, ML_DH), F32),
                  jnp.zeros((bp, ML_HEADS, ML_DH), F32),
                  jnp.zeros((bp, ML_HEADS), F32))
    weights = (norm_pre, norm_post, w_in, dn_conv_w, dn_A_log, dn_dt_bias, dn_norm, ssm_lam_re, ssm_lam_im,
               ssm_log_dt, ssm_B_re, ssm_B_im, ssm_C_re, ssm_C_im, ssm_D, ssm_glu_w, ssm_glu_b, ml_i_bias,
               ml_f_bias, ml_norm, w_out)
    xp, xs = x_prompt, x_sample
    new_p, new_s = [], []
    for l in range(DEPTH):
        p = _prep_layer_weights(tuple(w[l] for w in weights))
        xp, sp = _layer(xp, zero_state, p, _s5_tables(p, lp), lp, min(1024, bp * tp))
        carried = (state_dn_conv[l], state_dn_S[l], state_ssm_re[l], state_ssm_im[l],
                   state_ml_C[l], state_ml_n[l], state_ml_m[l])
        xs, ss = _layer(xs, carried, p, _s5_tables(p, ls), ls, min(1024, bs * ts))
        new_p.append(sp)
        new_s.append(ss)
    stack = lambda states, i, dt: jnp.stack([s[i] for s in states], axis=0).astype(dt)
    pst = [stack(new_p, i, x_prompt.dtype) for i in range(7)]
    sst = [stack(new_s, i, x_sample.dtype) for i in range(7)]
    return (xp, xs, pst[0], pst[1], pst[2], pst[3], pst[4], pst[5], pst[6],
            sst[0], sst[1], sst[2], sst[3], sst[4], sst[5], sst[6])
```

```python
import functools
import math

import jax
import jax.numpy as jnp
from jax import lax
from jax.experimental import pallas as pl
from jax.experimental.pallas import tpu as pltpu

F32 = jnp.float32
BF16 = jnp.bfloat16
HIGHEST = lax.Precision.HIGHEST

D_MODEL = 1024
DEPTH = 2
CHUNK = 64
D_MIX = 2 * D_MODEL
D_A = D_MIX // 2
DN_DK = 128
DN_DV = 128
DN_HEADS = D_A // DN_DV
CONV_W = 4
D_B = D_MIX // 4
SSM_GROUP = 16
SSM_GROUPS = D_B // SSM_GROUP
SSM_STATE = 64
D_C = D_MIX // 4
ML_DH = 128
ML_HEADS = D_C // ML_DH
EPS = 1e-6
N_SSM = SSM_GROUPS * SSM_STATE
S5_HALVES = 2
S5_ROWS = 256

LANES = 128
SUBLANES = 8

COL_SMALL = 3 * D_A + D_A + 2 * D_B + 5 * D_C
DP = 8192
SM_BETA = 0
SM_ALPHA = DN_HEADS
SM_I = 2 * DN_HEADS
SM_F = 2 * DN_HEADS + ML_HEADS

VMEM_LIMIT = 48 * 1024 * 1024


def _dot(a, b):
    return jnp.dot(a.astype(BF16), b.astype(BF16), preferred_element_type=F32)


def _dot_nt(a, b):
    return lax.dot_general(a.astype(BF16), b.astype(BF16), (((1,), (1,)), ((), ())),
                           preferred_element_type=F32)


def _dot_tn(a, b):
    return lax.dot_general(a.astype(BF16), b.astype(BF16), (((0,), (0,)), ((), ())),
                           preferred_element_type=F32)


def _bdot(a, b):
    return lax.dot_general(a.astype(BF16), b.astype(BF16), (((2,), (1,)), ((0,), (0,))),
                           preferred_element_type=F32)


def _bdot_nt(a, b):
    return lax.dot_general(a.astype(BF16), b.astype(BF16), (((2,), (2,)), ((0,), (0,))),
                           preferred_element_type=F32)


def _bdot_tn(a, b):
    return lax.dot_general(a.astype(BF16), b.astype(BF16), (((1,), (1,)), ((0,), (0,))),
                           preferred_element_type=F32)


def _dot_exact(a, b):
    return jnp.dot(a, b, precision=HIGHEST, preferred_element_type=F32)


def _transpose_exact(x):
    n = x.shape[1]
    eye = (lax.broadcasted_iota(jnp.int32, (n, n), 0) == lax.broadcasted_iota(jnp.int32, (n, n), 1)).astype(F32)
    return lax.dot_general(eye, x, (((1,), (1,)), ((), ())), precision=HIGHEST, preferred_element_type=F32)


def _sigmoid(x):
    return 1.0 / (1.0 + jnp.exp(-x))


def _silu(x):
    return x * _sigmoid(x)


def _softplus(x):
    return jnp.maximum(x, 0.0) + jnp.log(1.0 + jnp.exp(-jnp.abs(x)))


def _tri_masks(L):
    row = lax.broadcasted_iota(jnp.int32, (L, L), 0)
    col = lax.broadcasted_iota(jnp.int32, (L, L), 1)
    return row, col, row >= col, row > col


def _rms(x, g):
    return x * lax.rsqrt(jnp.mean(x * x, axis=-1, keepdims=True) + EPS) * g


def _proj_kernel(x_ref, g_ref, w_ref, o_ref, h_scr):
    @pl.when(pl.program_id(1) == 0)
    def _():
        h_scr[...] = _rms(x_ref[...], g_ref[...]).astype(BF16)

    o_ref[...] = jnp.dot(h_scr[...], w_ref[...], preferred_element_type=F32)


def _proj(x2, g_row, w_bf, tm, tn):
    n = x2.shape[0]
    return pl.pallas_call(
        _proj_kernel,
        grid=(n // tm, DP // tn),
        in_specs=[pl.BlockSpec((tm, D_MODEL), lambda i, j: (i, 0)),
                  pl.BlockSpec((1, D_MODEL), lambda i, j: (0, 0)),
                  pl.BlockSpec((D_MODEL, tn), lambda i, j: (0, j))],
        out_specs=pl.BlockSpec((tm, tn), lambda i, j: (i, j)),
        out_shape=jax.ShapeDtypeStruct((n, DP), F32),
        scratch_shapes=[pltpu.VMEM((tm, D_MODEL), BF16)],
        compiler_params=pltpu.CompilerParams(dimension_semantics=("parallel", "arbitrary"),
                                             vmem_limit_bytes=VMEM_LIMIT),
        name="proj",
    )(x2, g_row, w_bf)


def _unit_lower_inverse(a_strict, row, col, L):
    t = None
    s = 1
    while s < L:
        sh = s.bit_length() - 1
        m = ((row >> (sh + 1)) == (col >> (sh + 1))) & (((row >> sh) & 1) == 1) & (((col >> sh) & 1) == 0)
        a_off = jnp.where(m, a_strict, 0.0)
        if t is None:
            t = jnp.where(row == col, 1.0, 0.0) - a_off
        else:
            t = t - _bdot(t, _bdot(a_off, t))
        s *= 2
    return t


def _dn_kernel(q_ref, k_ref, v_ref, z_ref, sm_ref, conv0_ref, s0_ref, cw_ref, alog_ref, dtb_ref, nrm_ref,
               oa_ref, sout_ref, cout_ref, s_scr, xp_scr, *, L):
    n = pl.program_id(1)
    nc = pl.num_programs(1)

    @pl.when(n == 0)
    def _():
        s_scr[...] = s0_ref[0]
        xp_scr[0:SUBLANES, :] = conv0_ref[0]

    xp_scr[SUBLANES:SUBLANES + L, 0:D_A] = q_ref[0]
    xp_scr[SUBLANES:SUBLANES + L, D_A:2 * D_A] = k_ref[0]
    xp_scr[SUBLANES:SUBLANES + L, 2 * D_A:3 * D_A] = v_ref[0]

    row, col, incl, strict = _tri_masks(L)
    tri = incl.astype(F32)

    sm = sm_ref[0]
    beta_all = _sigmoid(sm)
    la_all = -jnp.exp(alog_ref[...]) * _softplus(sm + dtb_ref[...])
    g_all = _dot_exact(tri, la_all)
    g_t = _transpose_exact(g_all)

    def conv_silu(c0):
        base = SUBLANES - (CONV_W - 1)
        acc = None
        for j in range(CONV_W):
            term = xp_scr[base + j:base + j + L, c0:c0 + LANES] * cw_ref[j:j + 1, c0:c0 + LANES]
            acc = term if acc is None else acc + term
        return _silu(acc)

    heads = range(DN_HEADS)
    q = jnp.stack([conv_silu(h * DN_DK) for h in heads])
    k = jnp.stack([conv_silu(D_A + h * DN_DK) for h in heads])
    v = jnp.stack([conv_silu(2 * D_A + h * DN_DV) for h in heads])
    q = q * lax.rsqrt(jnp.sum(q * q, axis=-1, keepdims=True) + EPS) * (DN_DK ** -0.5)
    k = k * lax.rsqrt(jnp.sum(k * k, axis=-1, keepdims=True) + EPS)
    beta = jnp.stack([beta_all[:, SM_BETA + h:SM_BETA + h + 1] for h in heads])
    g = jnp.stack([g_all[:, SM_ALPHA + h:SM_ALPHA + h + 1] for h in heads])
    g_row = jnp.stack([g_t[SM_ALPHA + h:SM_ALPHA + h + 1, :] for h in heads])
    g_last = g[:, L - 1:L, :]
    eg = jnp.exp(g)
    decay = jnp.where(incl, jnp.exp(jnp.where(incl, g - g_row, 0.0)), 0.0)
    a_mat = jnp.where(strict, beta * _bdot_nt(k, k) * decay, 0.0)
    t_inv = _unit_lower_inverse(a_mat, row, col, L)
    rhs = jnp.concatenate([k * (beta * eg), v * beta], axis=-1)
    wu = _bdot(t_inv, rhs)
    w = wu[:, :, :DN_DK]
    u = wu[:, :, DN_DK:]
    attn = _bdot_nt(q, k) * decay
    s = s_scr[...]
    delta = u - _bdot(w, s)
    o = _bdot(q * eg, s) + _bdot(attn, delta)
    s_scr[...] = jnp.exp(g_last) * s + _bdot_tn(k * jnp.exp(g_last - g), delta)
    o = _rms(o, nrm_ref[...])
    for h in heads:
        oa_ref[0, :, h * DN_DV:(h + 1) * DN_DV] = o[h] * _silu(z_ref[0, :, h * DN_DV:(h + 1) * DN_DV])

    xp_scr[0:SUBLANES, :] = xp_scr[L:L + SUBLANES, :]

    @pl.when(n == nc - 1)
    def _():
        sout_ref[0] = s_scr[...]
        cout_ref[0] = xp_scr[0:SUBLANES, :]


def _dn(proj3, conv0p, s0, cw, alog_row, dtb_row, nrm_row, L):
    b, t, _ = proj3.shape
    nc = t // L
    col = lambda c: (lambda i, n: (i, n, c))
    fixed2 = lambda i, n: (0, 0)
    return pl.pallas_call(
        functools.partial(_dn_kernel, L=L),
        grid=(b, nc),
        in_specs=[pl.BlockSpec((1, L, D_A), col(0)),
                  pl.BlockSpec((1, L, D_A), col(1)),
                  pl.BlockSpec((1, L, D_A), col(2)),
                  pl.BlockSpec((1, L, D_A), col(3)),
                  pl.BlockSpec((1, L, LANES), col(COL_SMALL // LANES)),
                  pl.BlockSpec((1, SUBLANES, 3 * D_A), lambda i, n: (i, 0, 0)),
                  pl.BlockSpec((1, DN_HEADS, DN_DK, DN_DV), lambda i, n: (i, 0, 0, 0)),
                  pl.BlockSpec((CONV_W, 3 * D_A), fixed2),
                  pl.BlockSpec((1, LANES), fixed2),
                  pl.BlockSpec((1, LANES), fixed2),
                  pl.BlockSpec((1, DN_DV), fixed2)],
        out_specs=[pl.BlockSpec((1, L, D_A), lambda i, n: (i, n, 0)),
                   pl.BlockSpec((1, DN_HEADS, DN_DK, DN_DV), lambda i, n: (i, 0, 0, 0)),
                   pl.BlockSpec((1, SUBLANES, 3 * D_A), lambda i, n: (i, 0, 0))],
        out_shape=[jax.ShapeDtypeStruct((b, t, D_A), F32),
                   jax.ShapeDtypeStruct((b, DN_HEADS, DN_DK, DN_DV), F32),
                   jax.ShapeDtypeStruct((b, SUBLANES, 3 * D_A), F32)],
        scratch_shapes=[pltpu.VMEM((DN_HEADS, DN_DK, DN_DV), F32),
                        pltpu.VMEM((L + SUBLANES, 3 * D_A), F32)],
        compiler_params=pltpu.CompilerParams(dimension_semantics=("parallel", "arbitrary"),
                                             vmem_limit_bytes=VMEM_LIMIT),
        name="dn",
    )(proj3, proj3, proj3, proj3, proj3, conv0p, s0, cw, alog_row, dtb_row, nrm_row)


def _s5prep_kernel(a_ref, th_ref, lr_ref, li_ref, bre_ref, bim_ref, a16_ref, th16_ref,
                   pnr_ref, pni_ref, ppr_ref, ppi_ref, l1r_ref, l1i_ref, bbr_ref, bbi_ref, *, L):
    tcol = lax.broadcasted_iota(jnp.int32, (L, 1), 0).astype(F32)
    a = a_ref[...]
    th = th_ref[...]
    ppr_ref[...] = jnp.exp(a * tcol) * jnp.cos(th * tcol)
    ppi_ref[...] = jnp.exp(a * tcol) * jnp.sin(th * tcol)
    pnr_ref[...] = jnp.exp(-a * tcol) * jnp.cos(th * tcol)
    pni_ref[...] = -jnp.exp(-a * tcol) * jnp.sin(th * tcol)
    l1r_ref[...] = jnp.exp(a) * jnp.cos(th)
    l1i_ref[...] = jnp.exp(a) * jnp.sin(th)
    a16 = a16_ref[...]
    th16 = th16_ref[...]
    lr = lr_ref[...]
    li = li_ref[...]
    nr = jnp.exp(a16) * jnp.cos(th16) - 1.0
    ni = jnp.exp(a16) * jnp.sin(th16)
    den = lr * lr + li * li
    cr = (nr * lr + ni * li) / den
    ci = (ni * lr - nr * li) / den
    bre = bre_ref[...]
    bim = bim_ref[...]
    bbr_ref[...] = cr * bre - ci * bim
    bbi_ref[...] = cr * bim + ci * bre


def _s5prep(lam_re, lam_im, log_dt, b_re, b_im, L):
    dt = jnp.exp(log_dt)[:, None]
    a_row = (lam_re * dt).reshape(1, N_SSM)
    th_row = (lam_im * dt).reshape(1, N_SSM)
    rep = lambda x: jnp.broadcast_to(x.reshape(N_SSM, 1), (N_SSM, SSM_GROUP))
    outs = pl.pallas_call(
        functools.partial(_s5prep_kernel, L=L),
        out_shape=[jax.ShapeDtypeStruct((L, N_SSM), F32)] * 4
        + [jax.ShapeDtypeStruct((1, N_SSM), F32)] * 2
        + [jax.ShapeDtypeStruct((N_SSM, SSM_GROUP), F32)] * 2,
        name="s5prep",
    )(a_row, th_row, rep(lam_re), rep(lam_im), b_re.reshape(N_SSM, SSM_GROUP), b_im.reshape(N_SSM, SSM_GROUP),
      rep(lam_re * dt), rep(lam_im * dt))
    return outs


def _s5_kernel(u_ref, z_ref, h0r_ref, h0i_ref, pnr_ref, pni_ref, ppr_ref, ppi_ref, l1r_ref, l1i_ref,
               bre_ref, bim_ref, cre_ref, cim_ref, d_ref, gw_ref, gb_ref,
               ob_ref, hr_out, hi_out, hr_scr, hi_scr, *, L, TT):
    n = pl.program_id(1)
    nc = pl.num_programs(1)

    @pl.when(n == 0)
    def _():
        hr_scr[...] = h0r_ref[0]
        hi_scr[...] = h0i_ref[0]

    _, _, incl, _ = _tri_masks(L)
    tri = incl.astype(BF16)
    u = u_ref[0]
    ub = u.astype(BF16)
    halves = range(S5_HALVES)
    kh = D_B // S5_HALVES
    nh = N_SSM // S5_HALVES
    bur = jnp.concatenate([jnp.dot(ub[:, a * kh:(a + 1) * kh], bre_ref[a], preferred_element_type=F32)
                           for a in halves], axis=1)
    bui = jnp.concatenate([jnp.dot(ub[:, a * kh:(a + 1) * kh], bim_ref[a], preferred_element_type=F32)
                           for a in halves], axis=1)
    pnr = pnr_ref[...]
    pni = pni_ref[...]
    ppr = ppr_ref[...]
    ppi = ppi_ref[...]
    l1r = l1r_ref[...]
    l1i = l1i_ref[...]
    hr0 = hr_scr[...]
    hi0 = hi_scr[...]
    hrs, his = [], []
    for j in range(TT // L):
        rows = slice(j * L, (j + 1) * L)
        vr = pnr * bur[rows] - pni * bui[rows]
        vi = pnr * bui[rows] + pni * bur[rows]
        cr = _dot(tri, vr) + (l1r * hr0 - l1i * hi0)
        ci = _dot(tri, vi) + (l1r * hi0 + l1i * hr0)
        hr = ppr * cr - ppi * ci
        hi = ppr * ci + ppi * cr
        hr0 = hr[L - 1:L, :]
        hi0 = hi[L - 1:L, :]
        hrs.append(hr.astype(BF16))
        his.append(hi.astype(BF16))
    hr_scr[...] = hr0
    hi_scr[...] = hi0
    hrb = jnp.concatenate(hrs, axis=0)
    hib = jnp.concatenate(his, axis=0)
    y = jnp.concatenate([jnp.dot(hrb[:, a * nh:(a + 1) * nh], cre_ref[a], preferred_element_type=F32)
                         - jnp.dot(hib[:, a * nh:(a + 1) * nh], cim_ref[a], preferred_element_type=F32)
                         for a in halves], axis=1) + d_ref[...] * u
    c0 = math.sqrt(2.0 / math.pi)
    y = 0.5 * y * (1.0 + jnp.tanh(c0 * (y + 0.044715 * (y * y * y))))
    y = y * _sigmoid(_dot(y, gw_ref[...]) + gb_ref[...])
    ob_ref[0] = y * _silu(z_ref[0])

    @pl.when(n == nc - 1)
    def _():
        hr_out[0] = hr_scr[...]
        hi_out[0] = hi_scr[...]


def _s5(proj3, h0r, h0i, tabs, bbd_re, bbd_im, cbd_re, cbd_im, d_row, gw, gb_row, L):
    b, t, _ = proj3.shape
    tt = min(S5_ROWS, t)
    pnr, pni, ppr, ppi, l1r, l1i = tabs
    full = lambda a: pl.BlockSpec(a.shape, lambda i, n: (0,) * a.ndim)
    state = pl.BlockSpec((1, 1, N_SSM), lambda i, n: (i, 0, 0))
    u_col = (3 * D_A + D_A) // D_B
    return pl.pallas_call(
        functools.partial(_s5_kernel, L=L, TT=tt),
        grid=(b, t // tt),
        in_specs=[pl.BlockSpec((1, tt, D_B), lambda i, n: (i, n, u_col)),
                  pl.BlockSpec((1, tt, D_B), lambda i, n: (i, n, u_col + 1)),
                  state, state,
                  full(pnr), full(pni), full(ppr), full(ppi), full(l1r), full(l1i),
                  full(bbd_re), full(bbd_im), full(cbd_re), full(cbd_im), full(d_row), full(gw), full(gb_row)],
        out_specs=[pl.BlockSpec((1, tt, D_B), lambda i, n: (i, n, 0)), state, state],
        out_shape=[jax.ShapeDtypeStruct((b, t, D_B), F32),
                   jax.ShapeDtypeStruct((b, 1, N_SSM), F32),
                   jax.ShapeDtypeStruct((b, 1, N_SSM), F32)],
        scratch_shapes=[pltpu.VMEM((1, N_SSM), F32), pltpu.VMEM((1, N_SSM), F32)],
        compiler_params=pltpu.CompilerParams(dimension_semantics=("parallel", "arbitrary"),
                                             vmem_limit_bytes=VMEM_LIMIT),
        name="s5",
    )(proj3, proj3, h0r, h0i, pnr, pni, ppr, ppi, l1r, l1i, bbd_re, bbd_im, cbd_re, cbd_im, d_row, gw, gb_row)


def _ml_kernel(q_ref, k_ref, v_ref, o_ref, z_ref, sm_ref, c0_ref, n0_ref, m0_ref, ib_ref, fb_ref, nrm_ref,
               oc_ref, cout_ref, nout_ref, mout_ref, c_scr, n_scr, m_scr, *, L):
    n = pl.program_id(1)
    nc = pl.num_programs(1)

    @pl.when(n == 0)
    def _():
        c_scr[...] = c0_ref[0]
        n_scr[...] = n0_ref[0]
        m_scr[...] = m0_ref[0]

    row, col, incl, _ = _tri_masks(L)
    tri = incl.astype(F32)
    sm = sm_ref[0]
    ipre_all = sm + ib_ref[...]
    xf = sm + fb_ref[...]
    logf_all = jnp.minimum(xf, 0.0) - jnp.log(1.0 + jnp.exp(-jnp.abs(xf)))
    b_all = _dot_exact(tri, logf_all)
    b_t = _transpose_exact(b_all)
    i_t = _transpose_exact(ipre_all)
    lane = lax.broadcasted_iota(jnp.int32, (1, LANES), 1)
    m_all = m_scr[...]
    m_next = m_all

    heads = range(ML_HEADS)
    sl = lambda h: slice(h * ML_DH, (h + 1) * ML_DH)
    q = jnp.stack([q_ref[0, :, sl(h)] for h in heads])
    k = jnp.stack([k_ref[0, :, sl(h)] for h in heads]) * (ML_DH ** -0.5)
    v = jnp.stack([v_ref[0, :, sl(h)] for h in heads])
    b = jnp.stack([b_all[:, SM_F + h:SM_F + h + 1] for h in heads])
    b_row = jnp.stack([b_t[SM_F + h:SM_F + h + 1, :] for h in heads])
    i_col = jnp.stack([ipre_all[:, SM_I + h:SM_I + h + 1] for h in heads])
    i_row = jnp.stack([i_t[SM_I + h:SM_I + h + 1, :] for h in heads])
    m_prev = jnp.stack([m_all[:, h:h + 1] for h in heads])
    logw = jnp.where(incl, b - b_row + i_row, -jnp.inf)
    lwm = jnp.max(logw, axis=-1, keepdims=True)
    m_t = jnp.maximum(b + m_prev, lwm)
    w_state = jnp.exp(b + m_prev - m_t)
    w_intra = jnp.exp(logw - m_t) * _bdot_nt(q, k)
    c = c_scr[...]
    nvec = jnp.stack([n_scr[h:h + 1, :] for h in heads])
    num = w_state * _bdot(q, c) + _bdot(w_intra, v)
    den = w_state * jnp.sum(q * nvec, axis=-1, keepdims=True) + jnp.sum(w_intra, axis=-1, keepdims=True)
    hh = num / jnp.maximum(jnp.abs(den), jnp.exp(-m_t))
    m_new = m_t[:, L - 1:L, :]
    b_last = b[:, L - 1:L, :]
    w_keep = jnp.exp(b_last + m_prev - m_new)
    w_end = jnp.exp(b_last - b + i_col - m_new)
    kw = k * w_end
    c_scr[...] = w_keep * c + _bdot_tn(kw, v)
    n_new = w_keep * nvec + jnp.sum(kw, axis=1, keepdims=True)
    for h in heads:
        n_scr[h:h + 1, :] = n_new[h]
        m_next = jnp.where(lane == h, m_new[h], m_next)
        gated = _sigmoid(o_ref[0, :, sl(h)]) * hh[h]
        oc_ref[0, :, sl(h)] = _rms(gated, nrm_ref[...]) * _silu(z_ref[0, :, sl(h)])

    m_scr[...] = m_next

    @pl.when(n == nc - 1)
    def _():
        cout_ref[0] = c_scr[...]
        nout_ref[0] = n_scr[...]
        mout_ref[0] = m_scr[...]


def _ml(proj3, c0, n0, m0p, ib_row, fb_row, nrm_row, L):
    b, t, _ = proj3.shape
    nc = t // L
    c_col = (3 * D_A + D_A + 2 * D_B) // D_C
    col = lambda c: (lambda i, n: (i, n, c))
    fixed2 = lambda i, n: (0, 0)
    cspec = pl.BlockSpec((1, ML_HEADS, ML_DH, ML_DH), lambda i, n: (i, 0, 0, 0))
    nspec = pl.BlockSpec((1, ML_HEADS, ML_DH), lambda i, n: (i, 0, 0))
    mspec = pl.BlockSpec((1, 1, LANES), lambda i, n: (i, 0, 0))
    return pl.pallas_call(
        functools.partial(_ml_kernel, L=L),
        grid=(b, nc),
        in_specs=[pl.BlockSpec((1, L, D_C), col(c_col + j)) for j in range(5)]
        + [pl.BlockSpec((1, L, LANES), col(COL_SMALL // LANES)),
           cspec, nspec, mspec,
           pl.BlockSpec((1, LANES), fixed2), pl.BlockSpec((1, LANES), fixed2), pl.BlockSpec((1, ML_DH), fixed2)],
        out_specs=[pl.BlockSpec((1, L, D_C), lambda i, n: (i, n, 0)), cspec, nspec, mspec],
        out_shape=[jax.ShapeDtypeStruct((b, t, D_C), F32),
                   jax.ShapeDtypeStruct((b, ML_HEADS, ML_DH, ML_DH), F32),
                   jax.ShapeDtypeStruct((b, ML_HEADS, ML_DH), F32),
                   jax.ShapeDtypeStruct((b, 1, LANES), F32)],
        scratch_shapes=[pltpu.VMEM((ML_HEADS, ML_DH, ML_DH), F32),
                        pltpu.VMEM((ML_HEADS, ML_DH), F32),
                        pltpu.VMEM((1, LANES), F32)],
        compiler_params=pltpu.CompilerParams(dimension_semantics=("parallel", "arbitrary"),
                                             vmem_limit_bytes=VMEM_LIMIT),
        name="ml",
    )(proj3, proj3, proj3, proj3, proj3, proj3, c0, n0, m0p, ib_row, fb_row, nrm_row)


def _out_kernel(oa_ref, ob_ref, oc_ref, wa_ref, wb_ref, wc_ref, x_ref, g_ref, y_ref):
    acc = jnp.dot(oa_ref[...].astype(BF16), wa_ref[...], preferred_element_type=F32)
    acc = acc + jnp.dot(ob_ref[...].astype(BF16), wb_ref[...], preferred_element_type=F32)
    acc = acc + jnp.dot(oc_ref[...].astype(BF16), wc_ref[...], preferred_element_type=F32)
    y_ref[...] = x_ref[...] + _rms(acc, g_ref[...])


def _outproj(oa, ob, oc, wa, wb, wc, x2, g_row, tm):
    n = x2.shape[0]
    rows = lambda w: pl.BlockSpec((tm, w), lambda i: (i, 0))
    full = lambda a: pl.BlockSpec(a.shape, lambda i: (0, 0))
    return pl.pallas_call(
        _out_kernel,
        grid=(n // tm,),
        in_specs=[rows(D_A), rows(D_B), rows(D_C), full(wa), full(wb), full(wc), rows(D_MODEL), full(g_row)],
        out_specs=rows(D_MODEL),
        out_shape=jax.ShapeDtypeStruct((n, D_MODEL), F32),
        compiler_params=pltpu.CompilerParams(dimension_semantics=("parallel",),
                                             vmem_limit_bytes=VMEM_LIMIT),
        name="outproj",
    )(oa, ob, oc, wa, wb, wc, x2, g_row)


def _pad_lanes(v, offset):
    return jnp.zeros((1, LANES), F32).at[0, offset:offset + v.shape[0]].set(v.astype(F32))


def _block_diag_halves(blocks):
    g, r, c = blocks.shape
    gh = g // S5_HALVES
    eye = jnp.eye(gh, dtype=blocks.dtype)
    bd = jnp.einsum("agrc,gh->agrhc", blocks.reshape(S5_HALVES, gh, r, c), eye)
    return bd.reshape(S5_HALVES, gh * r, gh * c).astype(BF16)


def _prep_layer_weights(w):
    (norm_pre, norm_post, w_in, dn_conv_w, dn_A_log, dn_dt_bias, dn_norm, ssm_lam_re, ssm_lam_im, ssm_log_dt,
     ssm_B_re, ssm_B_im, ssm_C_re, ssm_C_im, ssm_D, ssm_glu_w, ssm_glu_b, ml_i_bias, ml_f_bias, ml_norm, w_out) = w
    o = 0
    seg = {}
    for name, size in (("qkv", 3 * D_A), ("az", D_A), ("beta", DN_HEADS), ("alpha", DN_HEADS), ("bu", D_B),
                       ("bz", D_B), ("cq", D_C), ("ck", D_C), ("cv", D_C), ("co", D_C), ("cz", D_C),
                       ("ci", ML_HEADS), ("cf", ML_HEADS)):
        seg[name] = w_in[:, o:o + size]
        o += size
    n_small = 2 * DN_HEADS + 2 * ML_HEADS
    w_re = jnp.concatenate(
        [seg[k] for k in ("qkv", "az", "bu", "bz", "cq", "ck", "cv", "co", "cz", "beta", "alpha", "ci", "cf")]
        + [jnp.zeros((D_MODEL, DP - COL_SMALL - n_small), w_in.dtype)], axis=1).astype(BF16)
    cbd = lambda c: _block_diag_halves(jnp.swapaxes(c.astype(F32), 1, 2))
    return dict(
        norm_pre=norm_pre.reshape(1, D_MODEL).astype(F32), norm_post=norm_post.reshape(1, D_MODEL).astype(F32),
        w_in=w_re, conv_w=dn_conv_w.astype(F32),
        alog_row=_pad_lanes(dn_A_log, SM_ALPHA), dtb_row=_pad_lanes(dn_dt_bias, SM_ALPHA),
        dn_norm=dn_norm.reshape(1, DN_DV).astype(F32),
        lam_re=ssm_lam_re.astype(F32), lam_im=ssm_lam_im.astype(F32), log_dt=ssm_log_dt.astype(F32),
        b_re=ssm_B_re.astype(F32), b_im=ssm_B_im.astype(F32), cbd_re=cbd(ssm_C_re), cbd_im=cbd(ssm_C_im),
        d_row=ssm_D.reshape(1, D_B).astype(F32), glu_w=ssm_glu_w.astype(BF16),
        glu_b=ssm_glu_b.reshape(1, D_B).astype(F32),
        ib_row=_pad_lanes(ml_i_bias, SM_I), fb_row=_pad_lanes(ml_f_bias, SM_F),
        ml_norm=ml_norm.reshape(1, ML_DH).astype(F32),
        wo_a=w_out[:D_A].astype(BF16), wo_b=w_out[D_A:D_A + D_B].astype(BF16), wo_c=w_out[D_A + D_B:].astype(BF16))


def _s5_tables(p, L):
    pnr, pni, ppr, ppi, l1r, l1i, bbr, bbi = _s5prep(p["lam_re"], p["lam_im"], p["log_dt"], p["b_re"], p["b_im"], L)
    bbd = lambda bb: _block_diag_halves(jnp.swapaxes(bb.reshape(SSM_GROUPS, SSM_STATE, SSM_GROUP), 1, 2))
    return (pnr, pni, ppr, ppi, l1r, l1i), bbd(bbr), bbd(bbi)


def _layer(x, state, p, s5tab, L, tm):
    conv0, s0, h0r, h0i, c0, n0, m0 = state
    b, t, _ = x.shape
    x2 = x.reshape(b * t, D_MODEL)
    proj3 = _proj(x2, p["norm_pre"], p["w_in"], tm, 512).reshape(b, t, DP)

    conv0p = jnp.pad(conv0.astype(F32), ((0, 0), (SUBLANES - (CONV_W - 1), 0), (0, 0)))
    oa, s_new, conv_new = _dn(proj3, conv0p, s0.astype(F32), p["conv_w"], p["alog_row"], p["dtb_row"],
                              p["dn_norm"], L)
    tabs, bbd_re, bbd_im = s5tab
    ob, hr_new, hi_new = _s5(proj3, h0r.reshape(b, 1, N_SSM).astype(F32), h0i.reshape(b, 1, N_SSM).astype(F32),
                             tabs, bbd_re, bbd_im, p["cbd_re"], p["cbd_im"], p["d_row"], p["glu_w"], p["glu_b"], L)
    m0p = jnp.pad(m0.astype(F32), ((0, 0), (0, LANES - ML_HEADS))).reshape(b, 1, LANES)
    oc, c_new, n_new, m_new = _ml(proj3, c0.astype(F32), n0.astype(F32), m0p, p["ib_row"], p["fb_row"],
                                  p["ml_norm"], L)
    y2 = _outproj(oa.reshape(b * t, D_A), ob.reshape(b * t, D_B), oc.reshape(b * t, D_C),
                  p["wo_a"], p["wo_b"], p["wo_c"], x2, p["norm_post"], tm)
    new_state = (conv_new[:, SUBLANES - (CONV_W - 1):, :], s_new,
                 hr_new.reshape(b, SSM_GROUPS, SSM_STATE), hi_new.reshape(b, SSM_GROUPS, SSM_STATE),
                 c_new, n_new, m_new[:, 0, :ML_HEADS])
    return y2.reshape(b, t, D_MODEL), new_state


def kernel(x_prompt, x_sample, state_dn_conv, state_dn_S, state_ssm_re, state_ssm_im, state_ml_C, state_ml_n, state_ml_m, norm_pre, norm_post, w_in, dn_conv_w, dn_A_log, dn_dt_bias, dn_norm, ssm_lam_re, ssm_lam_im, ssm_log_dt, ssm_B_re, ssm_B_im, ssm_C_re, ssm_C_im, ssm_D, ssm_glu_w, ssm_glu_b, ml_i_bias, ml_f_bias, ml_norm, w_out):
    bp, tp, _ = x_prompt.shape
    bs, ts, _ = x_sample.shape
    lp = min(CHUNK, tp)
    ls = min(CHUNK, ts)
    zero_state = (jnp.zeros((bp, CONV_W - 1, 3 * D_A), F32),
                  jnp.zeros((bp, DN_HEADS, DN_DK, DN_DV), F32),
                  jnp.zeros((bp, SSM_GROUPS, SSM_STATE), F32),
                  jnp.zeros((bp, SSM_GROUPS, SSM_STATE), F32),
                  jnp.zeros((bp, ML_HEADS, ML_DH, ML_DH), F32),
                  jnp.zeros((bp, ML_HEADS, ML_DH), F32),
                  jnp.zeros((bp, ML_HEADS), F32))
    weights = (norm_pre, norm_post, w_in, dn_conv_w, dn_A_log, dn_dt_bias, dn_norm, ssm_lam_re, ssm_lam_im,
               ssm_log_dt, ssm_B_re, ssm_B_im, ssm_C_re, ssm_C_im, ssm_D, ssm_glu_w, ssm_glu_b, ml_i_bias,
               ml_f_bias, ml_norm, w_out)
    xp, xs = x_prompt, x_sample
    new_p, new_s = [], []
    for l in range(DEPTH):
        p = _prep_layer_weights(tuple(w[l] for w in weights))
        xp, sp = _layer(xp, zero_state, p, _s5_tables(p, lp), lp, min(1024, bp * tp))
        carried = (state_dn_conv[l], state_dn_S[l], state_ssm_re[l], state_ssm_im[l],
                   state_ml_C[l], state_ml_n[l], state_ml_m[l])
        xs, ss = _layer(xs, carried, p, _s5_tables(p, ls), ls, min(1024, bs * ts))
        new_p.append(sp)
        new_s.append(ss)
    stack = lambda states, i, dt: jnp.stack([s[i] for s in states], axis=0).astype(dt)
    pst = [stack(new_p, i, x_prompt.dtype) for i in range(7)]
    sst = [stack(new_s, i, x_sample.dtype) for i in range(7)]
    return (xp, xs, pst[0], pst[1], pst[2], pst[3], pst[4], pst[5], pst[6],
            sst[0], sst[1], sst[2], sst[3], sst[4], sst[5], sst[6])
```

```python
import functools
import math

import jax
import jax.numpy as jnp
from jax import lax
from jax.experimental import pallas as pl
from jax.experimental.pallas import tpu as pltpu

F32 = jnp.float32
BF16 = jnp.bfloat16
HIGHEST = lax.Precision.HIGHEST

D_MODEL = 1024
DEPTH = 2
CHUNK = 64
D_MIX = 2 * D_MODEL
D_A = D_MIX // 2
DN_DK = 128
DN_DV = 128
DN_HEADS = D_A // DN_DV
CONV_W = 4
D_B = D_MIX // 4
SSM_GROUP = 16
SSM_GROUPS = D_B // SSM_GROUP
SSM_STATE = 64
D_C = D_MIX // 4
ML_DH = 128
ML_HEADS = D_C // ML_DH
EPS = 1e-6
N_SSM = SSM_GROUPS * SSM_STATE
S5_HALVES = 2
S5_ROWS = 256
PROJ_ROWS = 256
OUT_ROWS = 1024
DN_ROWS = 2
ML_ROWS = 4

LANES = 128
SUBLANES = 8

COL_Q, COL_K, COL_V = 0, D_A, 2 * D_A
COL_AZ = 3 * D_A
COL_BU = COL_AZ + D_A
COL_BZ = COL_BU + D_B
COL_CQ = COL_BZ + D_B
COL_SMALL = COL_CQ + 5 * D_C
DP = COL_SMALL + LANES
SM_BETA = 0
SM_ALPHA = DN_HEADS
SM_I = 2 * DN_HEADS
SM_F = 2 * DN_HEADS + ML_HEADS
PROJ_TILE = 512
_PROJ_TILES = tuple(
    [(COL_Q + c, PROJ_TILE, "q") for c in range(0, D_A, PROJ_TILE)]
    + [(COL_K + c, PROJ_TILE, "k") for c in range(0, D_A, PROJ_TILE)]
    + [(COL_V + c, PROJ_TILE, "v") for c in range(0, D_A, PROJ_TILE)]
    + [(COL_AZ + c, PROJ_TILE, "silu") for c in range(0, D_A, PROJ_TILE)]
    + [(COL_BU, D_B, "none"), (COL_BZ, D_B, "silu"),
       (COL_CQ, D_C, "none"), (COL_CQ + D_C, D_C, "kscale"), (COL_CQ + 2 * D_C, D_C, "none"),
       (COL_CQ + 3 * D_C, D_C, "sigmoid"), (COL_CQ + 4 * D_C, D_C, "silu"),
       (COL_SMALL, LANES, "none")])

VMEM_LIMIT = 48 * 1024 * 1024


def _dot(a, b):
    return jnp.dot(a.astype(BF16), b.astype(BF16), preferred_element_type=F32)


def _bdot(a, b):
    return lax.dot_general(a.astype(BF16), b.astype(BF16), (((2,), (1,)), ((0,), (0,))),
                           preferred_element_type=F32)


def _bdot_nt(a, b):
    return lax.dot_general(a.astype(BF16), b.astype(BF16), (((2,), (2,)), ((0,), (0,))),
                           preferred_element_type=F32)


def _bdot_tn(a, b):
    return lax.dot_general(a.astype(BF16), b.astype(BF16), (((1,), (1,)), ((0,), (0,))),
                           preferred_element_type=F32)


def _dot_exact(a, b):
    return jnp.dot(a, b, precision=HIGHEST, preferred_element_type=F32)


def _transpose_exact(x):
    n = x.shape[1]
    eye = (lax.broadcasted_iota(jnp.int32, (n, n), 0) == lax.broadcasted_iota(jnp.int32, (n, n), 1)).astype(F32)
    return lax.dot_general(eye, x, (((1,), (1,)), ((), ())), precision=HIGHEST, preferred_element_type=F32)


def _sigmoid(x):
    return 1.0 / (1.0 + jnp.exp(-x))


def _silu(x):
    return x * _sigmoid(x)


def _softplus(x):
    return jnp.maximum(x, 0.0) + jnp.log(1.0 + jnp.exp(-jnp.abs(x)))


def _tri_masks(L):
    row = lax.broadcasted_iota(jnp.int32, (L, L), 0)
    col = lax.broadcasted_iota(jnp.int32, (L, L), 1)
    return row, col, row >= col, row > col


def _rms(x, g):
    return x * lax.rsqrt(jnp.mean(x * x, axis=-1, keepdims=True) + EPS) * g


def _proj_kernel(x_ref, g_ref, w_ref, cw_ref, conv0_ref, o_ref, ctail_ref, ext_scr, carry_scr, *, tm, tiles_per_seq):
    @pl.when(pl.program_id(0) % tiles_per_seq == 0)
    def _():
        carry_scr[...] = conv0_ref[0]

    h = _rms(x_ref[...], g_ref[...]).astype(BF16)
    base = SUBLANES - (CONV_W - 1)
    for c0, width, kind in _PROJ_TILES:
        cols = slice(c0, c0 + width)
        acc = jnp.dot(h, w_ref[:, cols], preferred_element_type=F32)
        if kind in ("q", "k", "v"):
            ext_scr[0:SUBLANES, :] = carry_scr[:, cols]
            ext_scr[SUBLANES:SUBLANES + tm, :] = acc
            tail = acc[tm - SUBLANES:tm, :]
            carry_scr[:, cols] = tail
            ctail_ref[0, :, cols] = tail
            y = None
            for j in range(CONV_W):
                term = ext_scr[base + j:base + j + tm, :] * cw_ref[j:j + 1, cols]
                y = term if y is None else y + term
            y = _silu(y)
            if kind != "v":
                parts = []
                for s in range(width // DN_DK):
                    yh = y[:, s * DN_DK:(s + 1) * DN_DK]
                    yh = yh * lax.rsqrt(jnp.sum(yh * yh, axis=-1, keepdims=True) + EPS)
                    parts.append(yh * (DN_DK ** -0.5) if kind == "q" else yh)
                y = jnp.concatenate(parts, axis=1)
            o_ref[:, cols] = y
        elif kind == "silu":
            o_ref[:, cols] = _silu(acc)
        elif kind == "sigmoid":
            o_ref[:, cols] = _sigmoid(acc)
        elif kind == "kscale":
            o_ref[:, cols] = acc * (ML_DH ** -0.5)
        else:
            o_ref[:, cols] = acc


def _proj(x2, g_row, w_bf, cw, conv0p, tm, seq_len):
    n = x2.shape[0]
    nseq = conv0p.shape[0]
    tps = seq_len // tm
    return pl.pallas_call(
        functools.partial(_proj_kernel, tm=tm, tiles_per_seq=tps),
        grid=(n // tm,),
        in_specs=[pl.BlockSpec((tm, D_MODEL), lambda i: (i, 0)),
                  pl.BlockSpec((1, D_MODEL), lambda i: (0, 0)),
                  pl.BlockSpec((D_MODEL, DP), lambda i: (0, 0), pipeline_mode=pl.Buffered(1)),
                  pl.BlockSpec((CONV_W, 3 * D_A), lambda i: (0, 0)),
                  pl.BlockSpec((1, SUBLANES, 3 * D_A), lambda i: (i // tps, 0, 0))],
        out_specs=[pl.BlockSpec((tm, DP), lambda i: (i, 0)),
                   pl.BlockSpec((1, SUBLANES, 3 * D_A), lambda i: (i // tps, 0, 0))],
        out_shape=[jax.ShapeDtypeStruct((n, DP), F32),
                   jax.ShapeDtypeStruct((nseq, SUBLANES, 3 * D_A), F32)],
        scratch_shapes=[pltpu.VMEM((tm + SUBLANES, PROJ_TILE), F32),
                        pltpu.VMEM((SUBLANES, 3 * D_A), F32)],
        compiler_params=pltpu.CompilerParams(dimension_semantics=("arbitrary",),
                                             vmem_limit_bytes=VMEM_LIMIT),
        name="proj",
    )(x2, g_row, w_bf, cw, conv0p)


def _unit_lower_inverse(a_strict, row, col, L):
    t = None
    s = 1
    while s < L:
        sh = s.bit_length() - 1
        m = ((row >> (sh + 1)) == (col >> (sh + 1))) & (((row >> sh) & 1) == 1) & (((col >> sh) & 1) == 0)
        a_off = jnp.where(m, a_strict, 0.0)
        if t is None:
            t = jnp.where(row == col, 1.0, 0.0) - a_off
        else:
            t = t - _bdot(t, _bdot(a_off, t))
        s *= 2
    return t


def _dn_kernel(q_ref, k_ref, v_ref, z_ref, sm_ref, s0_ref, alog_ref, dtb_ref, nrm_ref,
               oa_ref, sout_ref, s_scr, *, L, BB):
    n = pl.program_id(1)
    nc = pl.num_programs(1)

    @pl.when(n == 0)
    def _():
        for r in range(BB):
            s_scr[r * DN_HEADS:(r + 1) * DN_HEADS] = s0_ref[r]

    row, col, incl, strict = _tri_masks(L)
    tri = incl.astype(F32)

    beta_l, g_l, grow_l = [], [], []
    for r in range(BB):
        sm = sm_ref[r]
        beta_all = _sigmoid(sm)
        la_all = -jnp.exp(alog_ref[...]) * _softplus(sm + dtb_ref[...])
        g_all = _dot_exact(tri, la_all)
        g_t = _transpose_exact(g_all)
        for h in range(DN_HEADS):
            beta_l.append(beta_all[:, SM_BETA + h:SM_BETA + h + 1])
            g_l.append(g_all[:, SM_ALPHA + h:SM_ALPHA + h + 1])
            grow_l.append(g_t[SM_ALPHA + h:SM_ALPHA + h + 1, :])

    pairs = [(r, h) for r in range(BB) for h in range(DN_HEADS)]
    hs = lambda h: slice(h * DN_DK, (h + 1) * DN_DK)
    q = jnp.stack([q_ref[r, :, hs(h)] for r, h in pairs])
    k = jnp.stack([k_ref[r, :, hs(h)] for r, h in pairs])
    v = jnp.stack([v_ref[r, :, hs(h)] for r, h in pairs])
    beta = jnp.stack(beta_l)
    g = jnp.stack(g_l)
    g_row = jnp.stack(grow_l)
    g_last = g[:, L - 1:L, :]
    eg = jnp.exp(g)
    decay = jnp.where(incl, jnp.exp(jnp.where(incl, g - g_row, 0.0)), 0.0)
    a_mat = jnp.where(strict, beta * _bdot_nt(k, k) * decay, 0.0)
    t_inv = _unit_lower_inverse(a_mat, row, col, L)
    rhs = jnp.concatenate([k * (beta * eg), v * beta], axis=-1)
    wu = _bdot(t_inv, rhs)
    w = wu[:, :, :DN_DK]
    u = wu[:, :, DN_DK:]
    attn = _bdot_nt(q, k) * decay
    s = s_scr[...]
    delta = u - _bdot(w, s)
    o = _bdot(q * eg, s) + _bdot(attn, delta)
    s_scr[...] = jnp.exp(g_last) * s + _bdot_tn(k * jnp.exp(g_last - g), delta)
    o = _rms(o, nrm_ref[...])
    for idx, (r, h) in enumerate(pairs):
        oa_ref[r, :, hs(h)] = o[idx] * z_ref[r, :, hs(h)]

    @pl.when(n == nc - 1)
    def _():
        for r in range(BB):
            sout_ref[r] = s_scr[r * DN_HEADS:(r + 1) * DN_HEADS]


def _dn(proj3, s0, alog_row, dtb_row, nrm_row, L):
    b, t, _ = proj3.shape
    bb = DN_ROWS
    col = lambda c: (lambda i, n: (i, n, c))
    fixed2 = lambda i, n: (0, 0)
    sspec = pl.BlockSpec((bb, DN_HEADS, DN_DK, DN_DV), lambda i, n: (i, 0, 0, 0))
    return pl.pallas_call(
        functools.partial(_dn_kernel, L=L, BB=bb),
        grid=(b // bb, t // L),
        in_specs=[pl.BlockSpec((bb, L, D_A), col(COL_Q // D_A)),
                  pl.BlockSpec((bb, L, D_A), col(COL_K // D_A)),
                  pl.BlockSpec((bb, L, D_A), col(COL_V // D_A)),
                  pl.BlockSpec((bb, L, D_A), col(COL_AZ // D_A)),
                  pl.BlockSpec((bb, L, LANES), col(COL_SMALL // LANES)),
                  sspec,
                  pl.BlockSpec((1, LANES), fixed2),
                  pl.BlockSpec((1, LANES), fixed2),
                  pl.BlockSpec((1, DN_DV), fixed2)],
        out_specs=[pl.BlockSpec((bb, L, D_A), lambda i, n: (i, n, 0)), sspec],
        out_shape=[jax.ShapeDtypeStruct((b, t, D_A), F32),
                   jax.ShapeDtypeStruct((b, DN_HEADS, DN_DK, DN_DV), F32)],
        scratch_shapes=[pltpu.VMEM((bb * DN_HEADS, DN_DK, DN_DV), F32)],
        compiler_params=pltpu.CompilerParams(dimension_semantics=("parallel", "arbitrary"),
                                             vmem_limit_bytes=VMEM_LIMIT),
        name="dn",
    )(proj3, proj3, proj3, proj3, proj3, s0, alog_row, dtb_row, nrm_row)


def _s5prep_kernel(a_ref, th_ref, lr_ref, li_ref, bre_ref, bim_ref, a16_ref, th16_ref,
                   pnr_ref, pni_ref, ppr_ref, ppi_ref, l1r_ref, l1i_ref, bbr_ref, bbi_ref, *, L):
    tcol = lax.broadcasted_iota(jnp.int32, (L, 1), 0).astype(F32)
    a = a_ref[...]
    th = th_ref[...]
    ppr_ref[...] = jnp.exp(a * tcol) * jnp.cos(th * tcol)
    ppi_ref[...] = jnp.exp(a * tcol) * jnp.sin(th * tcol)
    pnr_ref[...] = jnp.exp(-a * tcol) * jnp.cos(th * tcol)
    pni_ref[...] = -jnp.exp(-a * tcol) * jnp.sin(th * tcol)
    l1r_ref[...] = jnp.exp(a) * jnp.cos(th)
    l1i_ref[...] = jnp.exp(a) * jnp.sin(th)
    a16 = a16_ref[...]
    th16 = th16_ref[...]
    lr = lr_ref[...]
    li = li_ref[...]
    nr = jnp.exp(a16) * jnp.cos(th16) - 1.0
    ni = jnp.exp(a16) * jnp.sin(th16)
    den = lr * lr + li * li
    cr = (nr * lr + ni * li) / den
    ci = (ni * lr - nr * li) / den
    bre = bre_ref[...]
    bim = bim_ref[...]
    bbr_ref[...] = cr * bre - ci * bim
    bbi_ref[...] = cr * bim + ci * bre


def _s5prep(lam_re, lam_im, log_dt, b_re, b_im, L):
    dt = jnp.exp(log_dt)[:, None]
    a_row = (lam_re * dt).reshape(1, N_SSM)
    th_row = (lam_im * dt).reshape(1, N_SSM)
    rep = lambda x: jnp.broadcast_to(x.reshape(N_SSM, 1), (N_SSM, SSM_GROUP))
    outs = pl.pallas_call(
        functools.partial(_s5prep_kernel, L=L),
        out_shape=[jax.ShapeDtypeStruct((L, N_SSM), F32)] * 4
        + [jax.ShapeDtypeStruct((1, N_SSM), F32)] * 2
        + [jax.ShapeDtypeStruct((N_SSM, SSM_GROUP), F32)] * 2,
        name="s5prep",
    )(a_row, th_row, rep(lam_re), rep(lam_im), b_re.reshape(N_SSM, SSM_GROUP), b_im.reshape(N_SSM, SSM_GROUP),
      rep(lam_re * dt), rep(lam_im * dt))
    return outs


def _s5_kernel(u_ref, z_ref, h0r_ref, h0i_ref, pnr_ref, pni_ref, ppr_ref, ppi_ref, l1r_ref, l1i_ref,
               bre_ref, bim_ref, cre_ref, cim_ref, d_ref, gw_ref, gb_ref,
               ob_ref, hr_out, hi_out, hr_scr, hi_scr, *, L, TT):
    n = pl.program_id(1)
    nc = pl.num_programs(1)

    @pl.when(n == 0)
    def _():
        hr_scr[...] = h0r_ref[0]
        hi_scr[...] = h0i_ref[0]

    _, _, incl, _ = _tri_masks(L)
    tri = incl.astype(BF16)
    u = u_ref[0]
    ub = u.astype(BF16)
    halves = range(S5_HALVES)
    kh = D_B // S5_HALVES
    nh = N_SSM // S5_HALVES
    bur = jnp.concatenate([jnp.dot(ub[:, a * kh:(a + 1) * kh], bre_ref[a], preferred_element_type=F32)
                           for a in halves], axis=1)
    bui = jnp.concatenate([jnp.dot(ub[:, a * kh:(a + 1) * kh], bim_ref[a], preferred_element_type=F32)
                           for a in halves], axis=1)
    pnr = pnr_ref[...]
    pni = pni_ref[...]
    ppr = ppr_ref[...]
    ppi = ppi_ref[...]
    l1r = l1r_ref[...]
    l1i = l1i_ref[...]
    hr0 = hr_scr[...]
    hi0 = hi_scr[...]
    hrs, his = [], []
    for j in range(TT // L):
        rows = slice(j * L, (j + 1) * L)
        vr = pnr * bur[rows] - pni * bui[rows]
        vi = pnr * bui[rows] + pni * bur[rows]
        cr = _dot(tri, vr) + (l1r * hr0 - l1i * hi0)
        ci = _dot(tri, vi) + (l1r * hi0 + l1i * hr0)
        hr = ppr * cr - ppi * ci
        hi = ppr * ci + ppi * cr
        hr0 = hr[L - 1:L, :]
        hi0 = hi[L - 1:L, :]
        hrs.append(hr.astype(BF16))
        his.append(hi.astype(BF16))
    hr_scr[...] = hr0
    hi_scr[...] = hi0
    hrb = jnp.concatenate(hrs, axis=0)
    hib = jnp.concatenate(his, axis=0)
    y = jnp.concatenate([jnp.dot(hrb[:, a * nh:(a + 1) * nh], cre_ref[a], preferred_element_type=F32)
                         - jnp.dot(hib[:, a * nh:(a + 1) * nh], cim_ref[a], preferred_element_type=F32)
                         for a in halves], axis=1) + d_ref[...] * u
    c0 = math.sqrt(2.0 / math.pi)
    y = 0.5 * y * (1.0 + jnp.tanh(c0 * (y + 0.044715 * (y * y * y))))
    y = y * _sigmoid(_dot(y, gw_ref[...]) + gb_ref[...])
    ob_ref[0] = y * z_ref[0]

    @pl.when(n == nc - 1)
    def _():
        hr_out[0] = hr_scr[...]
        hi_out[0] = hi_scr[...]


def _s5(proj3, h0r, h0i, tabs, bbd_re, bbd_im, cbd_re, cbd_im, d_row, gw, gb_row, L):
    b, t, _ = proj3.shape
    tt = min(S5_ROWS, t)
    pnr, pni, ppr, ppi, l1r, l1i = tabs
    full = lambda a: pl.BlockSpec(a.shape, lambda i, n: (0,) * a.ndim)
    state = pl.BlockSpec((1, 1, N_SSM), lambda i, n: (i, 0, 0))
    u_col = COL_BU // D_B
    return pl.pallas_call(
        functools.partial(_s5_kernel, L=L, TT=tt),
        grid=(b, t // tt),
        in_specs=[pl.BlockSpec((1, tt, D_B), lambda i, n: (i, n, u_col)),
                  pl.BlockSpec((1, tt, D_B), lambda i, n: (i, n, u_col + 1)),
                  state, state,
                  full(pnr), full(pni), full(ppr), full(ppi), full(l1r), full(l1i),
                  full(bbd_re), full(bbd_im), full(cbd_re), full(cbd_im), full(d_row), full(gw), full(gb_row)],
        out_specs=[pl.BlockSpec((1, tt, D_B), lambda i, n: (i, n, 0)), state, state],
        out_shape=[jax.ShapeDtypeStruct((b, t, D_B), F32),
                   jax.ShapeDtypeStruct((b, 1, N_SSM), F32),
                   jax.ShapeDtypeStruct((b, 1, N_SSM), F32)],
        scratch_shapes=[pltpu.VMEM((1, N_SSM), F32), pltpu.VMEM((1, N_SSM), F32)],
        compiler_params=pltpu.CompilerParams(dimension_semantics=("parallel", "arbitrary"),
                                             vmem_limit_bytes=VMEM_LIMIT),
        name="s5",
    )(proj3, proj3, h0r, h0i, pnr, pni, ppr, ppi, l1r, l1i, bbd_re, bbd_im, cbd_re, cbd_im, d_row, gw, gb_row)


def _ml_kernel(q_ref, k_ref, v_ref, o_ref, z_ref, sm_ref, c0_ref, n0_ref, m0_ref, ib_ref, fb_ref, nrm_ref,
               oc_ref, cout_ref, nout_ref, mout_ref, c_scr, n_scr, m_scr, *, L, BB):
    n = pl.program_id(1)
    nc = pl.num_programs(1)

    @pl.when(n == 0)
    def _():
        for r in range(BB):
            c_scr[r * ML_HEADS:(r + 1) * ML_HEADS] = c0_ref[r]
        n_scr[...] = n0_ref[...]
        m_scr[...] = m0_ref[...]

    row, col, incl, _ = _tri_masks(L)
    tri = incl.astype(F32)
    lane = lax.broadcasted_iota(jnp.int32, (1, LANES), 1)
    pairs = [(r, h) for r in range(BB) for h in range(ML_HEADS)]
    sl = lambda h: slice(h * ML_DH, (h + 1) * ML_DH)

    b_l, brow_l, icol_l, irow_l, mprev_l, nvec_l = [], [], [], [], [], []
    for r in range(BB):
        sm = sm_ref[r]
        ipre_all = sm + ib_ref[...]
        xf = sm + fb_ref[...]
        logf_all = jnp.minimum(xf, 0.0) - jnp.log(1.0 + jnp.exp(-jnp.abs(xf)))
        b_all = _dot_exact(tri, logf_all)
        b_t = _transpose_exact(b_all)
        i_t = _transpose_exact(ipre_all)
        m_all = m_scr[r]
        n_all = n_scr[r]
        for h in range(ML_HEADS):
            b_l.append(b_all[:, SM_F + h:SM_F + h + 1])
            brow_l.append(b_t[SM_F + h:SM_F + h + 1, :])
            icol_l.append(ipre_all[:, SM_I + h:SM_I + h + 1])
            irow_l.append(i_t[SM_I + h:SM_I + h + 1, :])
            mprev_l.append(m_all[:, h:h + 1])
            nvec_l.append(n_all[h:h + 1, :])

    q = jnp.stack([q_ref[r, :, sl(h)] for r, h in pairs])
    k = jnp.stack([k_ref[r, :, sl(h)] for r, h in pairs])
    v = jnp.stack([v_ref[r, :, sl(h)] for r, h in pairs])
    b = jnp.stack(b_l)
    b_row = jnp.stack(brow_l)
    i_col = jnp.stack(icol_l)
    i_row = jnp.stack(irow_l)
    m_prev = jnp.stack(mprev_l)
    nvec = jnp.stack(nvec_l)
    logw = jnp.where(incl, b - b_row + i_row, -jnp.inf)
    lwm = jnp.max(logw, axis=-1, keepdims=True)
    m_t = jnp.maximum(b + m_prev, lwm)
    w_state = jnp.exp(b + m_prev - m_t)
    w_intra = jnp.exp(logw - m_t) * _bdot_nt(q, k)
    c = c_scr[...]
    num = w_state * _bdot(q, c) + _bdot(w_intra, v)
    den = w_state * jnp.sum(q * nvec, axis=-1, keepdims=True) + jnp.sum(w_intra, axis=-1, keepdims=True)
    hh = num / jnp.maximum(jnp.abs(den), jnp.exp(-m_t))
    m_new = m_t[:, L - 1:L, :]
    b_last = b[:, L - 1:L, :]
    w_keep = jnp.exp(b_last + m_prev - m_new)
    w_end = jnp.exp(b_last - b + i_col - m_new)
    kw = k * w_end
    c_scr[...] = w_keep * c + _bdot_tn(kw, v)
    n_new = w_keep * nvec + jnp.sum(kw, axis=1, keepdims=True)
    for r in range(BB):
        m_next = m_scr[r]
        for h in range(ML_HEADS):
            idx = r * ML_HEADS + h
            n_scr[r, h:h + 1, :] = n_new[idx]
            m_next = jnp.where(lane == h, m_new[idx], m_next)
            gated = o_ref[r, :, sl(h)] * hh[idx]
            oc_ref[r, :, sl(h)] = _rms(gated, nrm_ref[...]) * z_ref[r, :, sl(h)]
        m_scr[r] = m_next

    @pl.when(n == nc - 1)
    def _():
        for r in range(BB):
            cout_ref[r] = c_scr[r * ML_HEADS:(r + 1) * ML_HEADS]
        nout_ref[...] = n_scr[...]
        mout_ref[...] = m_scr[...]


def _ml(proj3, c0, n0, m0p, ib_row, fb_row, nrm_row, L):
    b, t, _ = proj3.shape
    bb = ML_ROWS
    c_col = COL_CQ // D_C
    col = lambda c: (lambda i, n: (i, n, c))
    fixed2 = lambda i, n: (0, 0)
    cspec = pl.BlockSpec((bb, ML_HEADS, ML_DH, ML_DH), lambda i, n: (i, 0, 0, 0))
    nspec = pl.BlockSpec((bb, ML_HEADS, ML_DH), lambda i, n: (i, 0, 0))
    mspec = pl.BlockSpec((bb, 1, LANES), lambda i, n: (i, 0, 0))
    return pl.pallas_call(
        functools.partial(_ml_kernel, L=L, BB=bb),
        grid=(b // bb, t // L),
        in_specs=[pl.BlockSpec((bb, L, D_C), col(c_col + j)) for j in range(5)]
        + [pl.BlockSpec((bb, L, LANES), col(COL_SMALL // LANES)),
           cspec, nspec, mspec,
           pl.BlockSpec((1, LANES), fixed2), pl.BlockSpec((1, LANES), fixed2), pl.BlockSpec((1, ML_DH), fixed2)],
        out_specs=[pl.BlockSpec((bb, L, D_C), lambda i, n: (i, n, 0)), cspec, nspec, mspec],
        out_shape=[jax.ShapeDtypeStruct((b, t, D_C), F32),
                   jax.ShapeDtypeStruct((b, ML_HEADS, ML_DH, ML_DH), F32),
                   jax.ShapeDtypeStruct((b, ML_HEADS, ML_DH), F32),
                   jax.ShapeDtypeStruct((b, 1, LANES), F32)],
        scratch_shapes=[pltpu.VMEM((bb * ML_HEADS, ML_DH, ML_DH), F32),
                        pltpu.VMEM((bb, ML_HEADS, ML_DH), F32),
                        pltpu.VMEM((bb, 1, LANES), F32)],
        compiler_params=pltpu.CompilerParams(dimension_semantics=("parallel", "arbitrary"),
                                             vmem_limit_bytes=VMEM_LIMIT),
        name="ml",
    )(proj3, proj3, proj3, proj3, proj3, proj3, c0, n0, m0p, ib_row, fb_row, nrm_row)


def _out_kernel(oa_ref, ob_ref, oc_ref, wa_ref, wb_ref, wc_ref, x_ref, g_ref, y_ref):
    acc = jnp.dot(oa_ref[...].astype(BF16), wa_ref[...], preferred_element_type=F32)
    acc = acc + jnp.dot(ob_ref[...].astype(BF16), wb_ref[...], preferred_element_type=F32)
    acc = acc + jnp.dot(oc_ref[...].astype(BF16), wc_ref[...], preferred_element_type=F32)
    y_ref[...] = x_ref[...] + _rms(acc, g_ref[...])


def _outproj(oa, ob, oc, wa, wb, wc, x2, g_row, tm):
    n = x2.shape[0]
    rows = lambda w: pl.BlockSpec((tm, w), lambda i: (i, 0))
    full = lambda a: pl.BlockSpec(a.shape, lambda i: (0, 0))
    return pl.pallas_call(
        _out_kernel,
        grid=(n // tm,),
        in_specs=[rows(D_A), rows(D_B), rows(D_C), full(wa), full(wb), full(wc), rows(D_MODEL), full(g_row)],
        out_specs=rows(D_MODEL),
        out_shape=jax.ShapeDtypeStruct((n, D_MODEL), F32),
        compiler_params=pltpu.CompilerParams(dimension_semantics=("parallel",),
                                             vmem_limit_bytes=VMEM_LIMIT),
        name="outproj",
    )(oa, ob, oc, wa, wb, wc, x2, g_row)


def _pad_lanes(v, offset):
    return jnp.zeros((1, LANES), F32).at[0, offset:offset + v.shape[0]].set(v.astype(F32))


def _block_diag_halves(blocks):
    g, r, c = blocks.shape
    gh = g // S5_HALVES
    eye = jnp.eye(gh, dtype=blocks.dtype)
    bd = jnp.einsum("agrc,gh->agrhc", blocks.reshape(S5_HALVES, gh, r, c), eye)
    return bd.reshape(S5_HALVES, gh * r, gh * c).astype(BF16)


def _prep_layer_weights(w):
    (norm_pre, norm_post, w_in, dn_conv_w, dn_A_log, dn_dt_bias, dn_norm, ssm_lam_re, ssm_lam_im, ssm_log_dt,
     ssm_B_re, ssm_B_im, ssm_C_re, ssm_C_im, ssm_D, ssm_glu_w, ssm_glu_b, ml_i_bias, ml_f_bias, ml_norm, w_out) = w
    o = 0
    seg = {}
    for name, size in (("qkv", 3 * D_A), ("az", D_A), ("beta", DN_HEADS), ("alpha", DN_HEADS), ("bu", D_B),
                       ("bz", D_B), ("cq", D_C), ("ck", D_C), ("cv", D_C), ("co", D_C), ("cz", D_C),
                       ("ci", ML_HEADS), ("cf", ML_HEADS)):
        seg[name] = w_in[:, o:o + size]
        o += size
    n_small = 2 * DN_HEADS + 2 * ML_HEADS
    w_re = jnp.concatenate(
        [seg[k] for k in ("qkv", "az", "bu", "bz", "cq", "ck", "cv", "co", "cz", "beta", "alpha", "ci", "cf")]
        + [jnp.zeros((D_MODEL, LANES - n_small), w_in.dtype)], axis=1).astype(BF16)
    cbd = lambda c: _block_diag_halves(jnp.swapaxes(c.astype(F32), 1, 2))
    return dict(
        norm_pre=norm_pre.reshape(1, D_MODEL).astype(F32), norm_post=norm_post.reshape(1, D_MODEL).astype(F32),
        w_in=w_re, conv_w=dn_conv_w.astype(F32),
        alog_row=_pad_lanes(dn_A_log, SM_ALPHA), dtb_row=_pad_lanes(dn_dt_bias, SM_ALPHA),
        dn_norm=dn_norm.reshape(1, DN_DV).astype(F32),
        lam_re=ssm_lam_re.astype(F32), lam_im=ssm_lam_im.astype(F32), log_dt=ssm_log_dt.astype(F32),
        b_re=ssm_B_re.astype(F32), b_im=ssm_B_im.astype(F32), cbd_re=cbd(ssm_C_re), cbd_im=cbd(ssm_C_im),
        d_row=ssm_D.reshape(1, D_B).astype(F32), glu_w=ssm_glu_w.astype(BF16),
        glu_b=ssm_glu_b.reshape(1, D_B).astype(F32),
        ib_row=_pad_lanes(ml_i_bias, SM_I), fb_row=_pad_lanes(ml_f_bias, SM_F),
        ml_norm=ml_norm.reshape(1, ML_DH).astype(F32),
        wo_a=w_out[:D_A].astype(BF16), wo_b=w_out[D_A:D_A + D_B].astype(BF16), wo_c=w_out[D_A + D_B:].astype(BF16))


def _s5_tables(p, L):
    pnr, pni, ppr, ppi, l1r, l1i, bbr, bbi = _s5prep(p["lam_re"], p["lam_im"], p["log_dt"], p["b_re"], p["b_im"], L)
    bbd = lambda bb: _block_diag_halves(jnp.swapaxes(bb.reshape(SSM_GROUPS, SSM_STATE, SSM_GROUP), 1, 2))
    return (pnr, pni, ppr, ppi, l1r, l1i), bbd(bbr), bbd(bbi)


def _layer(x, state, p, s5tab, L):
    conv0, s0, h0r, h0i, c0, n0, m0 = state
    b, t, _ = x.shape
    x2 = x.reshape(b * t, D_MODEL)
    conv0p = jnp.pad(conv0.astype(F32), ((0, 0), (SUBLANES - (CONV_W - 1), 0), (0, 0)))
    proj2, conv_tail = _proj(x2, p["norm_pre"], p["w_in"], p["conv_w"], conv0p, min(PROJ_ROWS, t), t)
    proj3 = proj2.reshape(b, t, DP)

    oa, s_new = _dn(proj3, s0.astype(F32), p["alog_row"], p["dtb_row"], p["dn_norm"], L)
    tabs, bbd_re, bbd_im = s5tab
    ob, hr_new, hi_new = _s5(proj3, h0r.reshape(b, 1, N_SSM).astype(F32), h0i.reshape(b, 1, N_SSM).astype(F32),
                             tabs, bbd_re, bbd_im, p["cbd_re"], p["cbd_im"], p["d_row"], p["glu_w"], p["glu_b"], L)
    m0p = jnp.pad(m0.astype(F32), ((0, 0), (0, LANES - ML_HEADS))).reshape(b, 1, LANES)
    oc, c_new, n_new, m_new = _ml(proj3, c0.astype(F32), n0.astype(F32), m0p, p["ib_row"], p["fb_row"],
                                  p["ml_norm"], L)
    y2 = _outproj(oa.reshape(b * t, D_A), ob.reshape(b * t, D_B), oc.reshape(b * t, D_C),
                  p["wo_a"], p["wo_b"], p["wo_c"], x2, p["norm_post"], min(OUT_ROWS, b * t))
    new_state = (conv_tail[:, SUBLANES - (CONV_W - 1):, :], s_new,
                 hr_new.reshape(b, SSM_GROUPS, SSM_STATE), hi_new.reshape(b, SSM_GROUPS, SSM_STATE),
                 c_new, n_new, m_new[:, 0, :ML_HEADS])
    return y2.reshape(b, t, D_MODEL), new_state


def kernel(x_prompt, x_sample, state_dn_conv, state_dn_S, state_ssm_re, state_ssm_im, state_ml_C, state_ml_n, state_ml_m, norm_pre, norm_post, w_in, dn_conv_w, dn_A_log, dn_dt_bias, dn_norm, ssm_lam_re, ssm_lam_im, ssm_log_dt, ssm_B_re, ssm_B_im, ssm_C_re, ssm_C_im, ssm_D, ssm_glu_w, ssm_glu_b, ml_i_bias, ml_f_bias, ml_norm, w_out):
    bp, tp, _ = x_prompt.shape
    bs, ts, _ = x_sample.shape
    lp = min(CHUNK, tp)
    ls = min(CHUNK, ts)
    zero_state = (jnp.zeros((bp, CONV_W - 1, 3 * D_A), F32),
                  jnp.zeros((bp, DN_HEADS, DN_DK, DN_DV), F32),
                  jnp.zeros((bp, SSM_GROUPS, SSM_STATE), F32),
                  jnp.zeros((bp, SSM_GROUPS, SSM_STATE), F32),
                  jnp.zeros((bp, ML_HEADS, ML_DH, ML_DH), F32),
                  jnp.zeros((bp, ML_HEADS, ML_DH), F32),
                  jnp.zeros((bp, ML_HEADS), F32))
    weights = (norm_pre, norm_post, w_in, dn_conv_w, dn_A_log, dn_dt_bias, dn_norm, ssm_lam_re, ssm_lam_im,
               ssm_log_dt, ssm_B_re, ssm_B_im, ssm_C_re, ssm_C_im, ssm_D, ssm_glu_w, ssm_glu_b, ml_i_bias,
               ml_f_bias, ml_norm, w_out)
    xp, xs = x_prompt, x_sample
    new_p, new_s = [], []
    for l in range(DEPTH):
        p = _prep_layer_weights(tuple(w[l] for w in weights))
        xp, sp = _layer(xp, zero_state, p, _s5_tables(p, lp), lp)
        carried = (state_dn_conv[l], state_dn_S[l], state_ssm_re[l], state_ssm_im[l],
                   state_ml_C[l], state_ml_n[l], state_ml_m[l])
        xs, ss = _layer(xs, carried, p, _s5_tables(p, ls), ls)
        new_p.append(sp)
        new_s.append(ss)
    stack = lambda states, i, dt: jnp.stack([s[i] for s in states], axis=0).astype(dt)
    pst = [stack(new_p, i, x_prompt.dtype) for i in range(7)]
    sst = [stack(new_s, i, x_sample.dtype) for i in range(7)]
    return (xp, xs, pst[0], pst[1], pst[2], pst[3], pst[4], pst[5], pst[6],
            sst[0], sst[1], sst[2], sst[3], sst[4], sst[5], sst[6])
```

```python
import functools
import math

import jax
import jax.numpy as jnp
from jax import lax
from jax.experimental import pallas as pl
from jax.experimental.pallas import tpu as pltpu

F32 = jnp.float32
BF16 = jnp.bfloat16
HIGHEST = lax.Precision.HIGHEST

D_MODEL = 1024
DEPTH = 2
CHUNK = 64
D_MIX = 2 * D_MODEL
D_A = D_MIX // 2
DN_DK = 128
DN_DV = 128
DN_HEADS = D_A // DN_DV
CONV_W = 4
D_B = D_MIX // 4
SSM_GROUP = 16
SSM_GROUPS = D_B // SSM_GROUP
SSM_STATE = 64
D_C = D_MIX // 4
ML_DH = 128
ML_HEADS = D_C // ML_DH
EPS = 1e-6
N_SSM = SSM_GROUPS * SSM_STATE
S5_HALVES = 2
S5_ROWS = 256
S5_COLS = 256
PROJ_ROWS = 256
OUT_ROWS = 1024
DN_ROWS = 4
ML_ROWS = 4

LANES = 128
SUBLANES = 8

COL_Q, COL_K, COL_V = 0, D_A, 2 * D_A
COL_AZ = 3 * D_A
COL_BU = COL_AZ + D_A
COL_BZ = COL_BU + D_B
COL_CQ = COL_BZ + D_B
COL_SMALL = COL_CQ + 5 * D_C
DP = COL_SMALL + LANES
SM_BETA = 0
SM_ALPHA = DN_HEADS
SM_I = 2 * DN_HEADS
SM_F = 2 * DN_HEADS + ML_HEADS
PROJ_TILE = 512
_PROJ_TILES = tuple(
    [(COL_Q + c, PROJ_TILE, "q") for c in range(0, D_A, PROJ_TILE)]
    + [(COL_K + c, PROJ_TILE, "k") for c in range(0, D_A, PROJ_TILE)]
    + [(COL_V + c, PROJ_TILE, "v") for c in range(0, D_A, PROJ_TILE)]
    + [(COL_AZ + c, PROJ_TILE, "silu") for c in range(0, D_A, PROJ_TILE)]
    + [(COL_BU, D_B, "none"), (COL_BZ, D_B, "silu"),
       (COL_CQ, D_C, "none"), (COL_CQ + D_C, D_C, "kscale"), (COL_CQ + 2 * D_C, D_C, "none"),
       (COL_CQ + 3 * D_C, D_C, "sigmoid"), (COL_CQ + 4 * D_C, D_C, "silu"),
       (COL_SMALL, LANES, "none")])

VMEM_LIMIT = 48 * 1024 * 1024


def _dot(a, b):
    return jnp.dot(a.astype(BF16), b.astype(BF16), preferred_element_type=F32)


def _bdot(a, b):
    return lax.dot_general(a.astype(BF16), b.astype(BF16), (((2,), (1,)), ((0,), (0,))),
                           preferred_element_type=F32)


def _bdot_nt(a, b):
    return lax.dot_general(a.astype(BF16), b.astype(BF16), (((2,), (2,)), ((0,), (0,))),
                           preferred_element_type=F32)


def _bdot_tn(a, b):
    return lax.dot_general(a.astype(BF16), b.astype(BF16), (((1,), (1,)), ((0,), (0,))),
                           preferred_element_type=F32)


def _dot_exact(a, b):
    return jnp.dot(a, b, precision=HIGHEST, preferred_element_type=F32)


def _transpose_exact(x):
    n = x.shape[1]
    eye = (lax.broadcasted_iota(jnp.int32, (n, n), 0) == lax.broadcasted_iota(jnp.int32, (n, n), 1)).astype(F32)
    return lax.dot_general(eye, x, (((1,), (1,)), ((), ())), precision=HIGHEST, preferred_element_type=F32)


def _sigmoid(x):
    return 1.0 / (1.0 + jnp.exp(-x))


def _silu(x):
    return x * _sigmoid(x)


def _softplus(x):
    return jnp.maximum(x, 0.0) + jnp.log(1.0 + jnp.exp(-jnp.abs(x)))


def _tri_masks(L):
    row = lax.broadcasted_iota(jnp.int32, (L, L), 0)
    col = lax.broadcasted_iota(jnp.int32, (L, L), 1)
    return row, col, row >= col, row > col


def _rms(x, g):
    return x * lax.rsqrt(jnp.mean(x * x, axis=-1, keepdims=True) + EPS) * g


def _proj_kernel(x_ref, g_ref, w_ref, cw_ref, conv0_ref, o_ref, ctail_ref, carry_scr, *, tm, tiles_per_seq):
    @pl.when(pl.program_id(0) % tiles_per_seq == 0)
    def _():
        carry_scr[...] = conv0_ref[0]

    h = _rms(x_ref[...], g_ref[...]).astype(BF16)
    for c0, width, kind in _PROJ_TILES:
        cols = slice(c0, c0 + width)
        acc = jnp.dot(h, w_ref[:, cols], preferred_element_type=F32)
        if kind in ("q", "k", "v"):
            ext = jnp.concatenate([carry_scr[:, cols], acc], axis=0)
            tail = acc[tm - SUBLANES:tm, :]
            carry_scr[:, cols] = tail
            ctail_ref[0, :, cols] = tail
            y = acc * cw_ref[CONV_W - 1:CONV_W, cols]
            for d in range(1, CONV_W):
                shifted = pltpu.roll(ext, d, axis=0)[SUBLANES:SUBLANES + tm, :]
                y = y + shifted * cw_ref[CONV_W - 1 - d:CONV_W - d, cols]
            y = _silu(y)
            if kind != "v":
                parts = []
                for s in range(width // DN_DK):
                    yh = y[:, s * DN_DK:(s + 1) * DN_DK]
                    yh = yh * lax.rsqrt(jnp.sum(yh * yh, axis=-1, keepdims=True) + EPS)
                    parts.append(yh * (DN_DK ** -0.5) if kind == "q" else yh)
                y = jnp.concatenate(parts, axis=1)
            o_ref[:, cols] = y
        elif kind == "silu":
            o_ref[:, cols] = _silu(acc)
        elif kind == "sigmoid":
            o_ref[:, cols] = _sigmoid(acc)
        elif kind == "kscale":
            o_ref[:, cols] = acc * (ML_DH ** -0.5)
        else:
            o_ref[:, cols] = acc


def _proj(x2, g_row, w_bf, cw, conv0p, tm, seq_len):
    n = x2.shape[0]
    nseq = conv0p.shape[0]
    tps = seq_len // tm
    return pl.pallas_call(
        functools.partial(_proj_kernel, tm=tm, tiles_per_seq=tps),
        grid=(n // tm,),
        in_specs=[pl.BlockSpec((tm, D_MODEL), lambda i: (i, 0)),
                  pl.BlockSpec((1, D_MODEL), lambda i: (0, 0)),
                  pl.BlockSpec((D_MODEL, DP), lambda i: (0, 0), pipeline_mode=pl.Buffered(1)),
                  pl.BlockSpec((CONV_W, 3 * D_A), lambda i: (0, 0)),
                  pl.BlockSpec((1, SUBLANES, 3 * D_A), lambda i: (i // tps, 0, 0))],
        out_specs=[pl.BlockSpec((tm, DP), lambda i: (i, 0)),
                   pl.BlockSpec((1, SUBLANES, 3 * D_A), lambda i: (i // tps, 0, 0))],
        out_shape=[jax.ShapeDtypeStruct((n, DP), F32),
                   jax.ShapeDtypeStruct((nseq, SUBLANES, 3 * D_A), F32)],
        scratch_shapes=[pltpu.VMEM((SUBLANES, 3 * D_A), F32)],
        compiler_params=pltpu.CompilerParams(dimension_semantics=("arbitrary",),
                                             vmem_limit_bytes=VMEM_LIMIT),
        name="proj",
    )(x2, g_row, w_bf, cw, conv0p)


def _unit_lower_inverse(a_strict, row, col, L):
    t = None
    s = 1
    while s < L:
        sh = s.bit_length() - 1
        m = ((row >> (sh + 1)) == (col >> (sh + 1))) & (((row >> sh) & 1) == 1) & (((col >> sh) & 1) == 0)
        a_off = jnp.where(m, a_strict, 0.0)
        if t is None:
            t = jnp.where(row == col, 1.0, 0.0) - a_off
        else:
            t = t - _bdot(t, _bdot(a_off, t))
        s *= 2
    return t


def _dn_kernel(q_ref, k_ref, v_ref, z_ref, sm_ref, s0_ref, alog_ref, dtb_ref, nrm_ref,
               oa_ref, sout_ref, s_scr, *, L, BB):
    n = pl.program_id(1)
    nc = pl.num_programs(1)

    @pl.when(n == 0)
    def _():
        for r in range(BB):
            s_scr[r * DN_HEADS:(r + 1) * DN_HEADS] = s0_ref[r]

    row, col, incl, strict = _tri_masks(L)
    tri = incl.astype(F32)

    beta_l, g_l, grow_l = [], [], []
    for r in range(BB):
        sm = sm_ref[r]
        beta_all = _sigmoid(sm)
        la_all = -jnp.exp(alog_ref[...]) * _softplus(sm + dtb_ref[...])
        g_all = _dot_exact(tri, la_all)
        g_t = _transpose_exact(g_all)
        for h in range(DN_HEADS):
            beta_l.append(beta_all[:, SM_BETA + h:SM_BETA + h + 1])
            g_l.append(g_all[:, SM_ALPHA + h:SM_ALPHA + h + 1])
            grow_l.append(g_t[SM_ALPHA + h:SM_ALPHA + h + 1, :])

    pairs = [(r, h) for r in range(BB) for h in range(DN_HEADS)]
    hs = lambda h: slice(h * DN_DK, (h + 1) * DN_DK)
    q = jnp.stack([q_ref[r, :, hs(h)] for r, h in pairs])
    k = jnp.stack([k_ref[r, :, hs(h)] for r, h in pairs])
    v = jnp.stack([v_ref[r, :, hs(h)] for r, h in pairs])
    beta = jnp.stack(beta_l)
    g = jnp.stack(g_l)
    g_row = jnp.stack(grow_l)
    g_last = g[:, L - 1:L, :]
    eg = jnp.exp(g)
    decay = jnp.where(incl, jnp.exp(jnp.where(incl, g - g_row, 0.0)), 0.0)
    kq_k = _bdot_nt(jnp.concatenate([k, q], axis=1), k)
    a_mat = jnp.where(strict, beta * kq_k[:, :L, :] * decay, 0.0)
    attn = kq_k[:, L:, :] * decay
    t_inv = _unit_lower_inverse(a_mat, row, col, L)
    rhs = jnp.concatenate([k * (beta * eg), v * beta], axis=-1)
    wu = _bdot(t_inv, rhs)
    w = wu[:, :, :DN_DK]
    u = wu[:, :, DN_DK:]
    s = s_scr[...]
    wq_s = _bdot(jnp.concatenate([w, q * eg], axis=1), s)
    delta = u - wq_s[:, :L, :]
    o = wq_s[:, L:, :] + _bdot(attn, delta)
    s_scr[...] = jnp.exp(g_last) * s + _bdot_tn(k * jnp.exp(g_last - g), delta)
    o = _rms(o, nrm_ref[...])
    for idx, (r, h) in enumerate(pairs):
        oa_ref[r, :, hs(h)] = o[idx] * z_ref[r, :, hs(h)]

    @pl.when(n == nc - 1)
    def _():
        for r in range(BB):
            sout_ref[r] = s_scr[r * DN_HEADS:(r + 1) * DN_HEADS]


def _dn(proj3, s0, alog_row, dtb_row, nrm_row, L):
    b, t, _ = proj3.shape
    bb = DN_ROWS
    col = lambda c: (lambda i, n: (i, n, c))
    fixed2 = lambda i, n: (0, 0)
    sspec = pl.BlockSpec((bb, DN_HEADS, DN_DK, DN_DV), lambda i, n: (i, 0, 0, 0))
    return pl.pallas_call(
        functools.partial(_dn_kernel, L=L, BB=bb),
        grid=(b // bb, t // L),
        in_specs=[pl.BlockSpec((bb, L, D_A), col(COL_Q // D_A)),
                  pl.BlockSpec((bb, L, D_A), col(COL_K // D_A)),
                  pl.BlockSpec((bb, L, D_A), col(COL_V // D_A)),
                  pl.BlockSpec((bb, L, D_A), col(COL_AZ // D_A)),
                  pl.BlockSpec((bb, L, LANES), col(COL_SMALL // LANES)),
                  sspec,
                  pl.BlockSpec((1, LANES), fixed2),
                  pl.BlockSpec((1, LANES), fixed2),
                  pl.BlockSpec((1, DN_DV), fixed2)],
        out_specs=[pl.BlockSpec((bb, L, D_A), lambda i, n: (i, n, 0)), sspec],
        out_shape=[jax.ShapeDtypeStruct((b, t, D_A), F32),
                   jax.ShapeDtypeStruct((b, DN_HEADS, DN_DK, DN_DV), F32)],
        scratch_shapes=[pltpu.VMEM((bb * DN_HEADS, DN_DK, DN_DV), F32)],
        compiler_params=pltpu.CompilerParams(dimension_semantics=("parallel", "arbitrary"),
                                             vmem_limit_bytes=VMEM_LIMIT),
        name="dn",
    )(proj3, proj3, proj3, proj3, proj3, s0, alog_row, dtb_row, nrm_row)


def _s5prep_kernel(a_ref, th_ref, lr_ref, li_ref, bre_ref, bim_ref, a16_ref, th16_ref,
                   pnr_ref, pni_ref, ppr_ref, ppi_ref, l1r_ref, l1i_ref, bbr_ref, bbi_ref, *, L):
    tcol = lax.broadcasted_iota(jnp.int32, (L, 1), 0).astype(F32)
    a = a_ref[...]
    th = th_ref[...]
    ppr_ref[...] = jnp.exp(a * tcol) * jnp.cos(th * tcol)
    ppi_ref[...] = jnp.exp(a * tcol) * jnp.sin(th * tcol)
    pnr_ref[...] = jnp.exp(-a * tcol) * jnp.cos(th * tcol)
    pni_ref[...] = -jnp.exp(-a * tcol) * jnp.sin(th * tcol)
    l1r_ref[...] = jnp.exp(a) * jnp.cos(th)
    l1i_ref[...] = jnp.exp(a) * jnp.sin(th)
    a16 = a16_ref[...]
    th16 = th16_ref[...]
    lr = lr_ref[...]
    li = li_ref[...]
    nr = jnp.exp(a16) * jnp.cos(th16) - 1.0
    ni = jnp.exp(a16) * jnp.sin(th16)
    den = lr * lr + li * li
    cr = (nr * lr + ni * li) / den
    ci = (ni * lr - nr * li) / den
    bre = bre_ref[...]
    bim = bim_ref[...]
    bbr_ref[...] = cr * bre - ci * bim
    bbi_ref[...] = cr * bim + ci * bre


def _s5prep(lam_re, lam_im, log_dt, b_re, b_im, L):
    dt = jnp.exp(log_dt)[:, None]
    a_row = (lam_re * dt).reshape(1, N_SSM)
    th_row = (lam_im * dt).reshape(1, N_SSM)
    rep = lambda x: jnp.broadcast_to(x.reshape(N_SSM, 1), (N_SSM, SSM_GROUP))
    outs = pl.pallas_call(
        functools.partial(_s5prep_kernel, L=L),
        out_shape=[jax.ShapeDtypeStruct((L, N_SSM), F32)] * 4
        + [jax.ShapeDtypeStruct((1, N_SSM), F32)] * 2
        + [jax.ShapeDtypeStruct((N_SSM, SSM_GROUP), F32)] * 2,
        name="s5prep",
    )(a_row, th_row, rep(lam_re), rep(lam_im), b_re.reshape(N_SSM, SSM_GROUP), b_im.reshape(N_SSM, SSM_GROUP),
      rep(lam_re * dt), rep(lam_im * dt))
    return outs


def _s5_kernel(u_ref, z_ref, h0r_ref, h0i_ref, pnr_ref, pni_ref, ppr_ref, ppi_ref, l1r_ref, l1i_ref,
               bre_ref, bim_ref, cre_ref, cim_ref, d_ref, gw_ref, gb_ref,
               ob_ref, hr_out, hi_out, hr_scr, hi_scr, hbr_scr, hbi_scr, *, L, TT):
    n = pl.program_id(1)
    nc = pl.num_programs(1)

    @pl.when(n == 0)
    def _():
        hr_scr[...] = h0r_ref[0]
        hi_scr[...] = h0i_ref[0]

    _, _, incl, _ = _tri_masks(L)
    tri = incl.astype(BF16)
    u = u_ref[0]
    ub = u.astype(BF16)
    halves = range(S5_HALVES)
    kh = D_B // S5_HALVES
    nh = N_SSM // S5_HALVES
    ncb = N_SSM // S5_COLS
    bus = []
    for cb in range(ncb):
        a = (cb * S5_COLS) // nh
        lc = cb * S5_COLS - a * nh
        ua = ub[:, a * kh:(a + 1) * kh]
        bus.append((jnp.dot(ua, bre_ref[a, :, lc:lc + S5_COLS], preferred_element_type=F32),
                    jnp.dot(ua, bim_ref[a, :, lc:lc + S5_COLS], preferred_element_type=F32)))
    for cb in range(ncb):
        cs = slice(cb * S5_COLS, (cb + 1) * S5_COLS)
        bur, bui = bus[cb]
        l1r = l1r_ref[:, cs]
        l1i = l1i_ref[:, cs]
        hr0 = hr_scr[:, cs]
        hi0 = hi_scr[:, cs]
        for j in range(TT // L):
            rows = slice(j * L, (j + 1) * L)
            pnr = pnr_ref[:, cs]
            pni = pni_ref[:, cs]
            vr = pnr * bur[rows] - pni * bui[rows]
            vi = pnr * bui[rows] + pni * bur[rows]
            cr = _dot(tri, vr) + (l1r * hr0 - l1i * hi0)
            ci = _dot(tri, vi) + (l1r * hi0 + l1i * hr0)
            ppr = ppr_ref[:, cs]
            ppi = ppi_ref[:, cs]
            hr = ppr * cr - ppi * ci
            hi = ppr * ci + ppi * cr
            hr0 = hr[L - 1:L, :]
            hi0 = hi[L - 1:L, :]
            hbr_scr[rows, cs] = hr.astype(BF16)
            hbi_scr[rows, cs] = hi.astype(BF16)
        hr_scr[:, cs] = hr0
        hi_scr[:, cs] = hi0
    y = jnp.concatenate([jnp.dot(hbr_scr[:, a * nh:(a + 1) * nh], cre_ref[a], preferred_element_type=F32)
                         - jnp.dot(hbi_scr[:, a * nh:(a + 1) * nh], cim_ref[a], preferred_element_type=F32)
                         for a in halves], axis=1) + d_ref[...] * u
    c0 = math.sqrt(2.0 / math.pi)
    y = 0.5 * y * (1.0 + jnp.tanh(c0 * (y + 0.044715 * (y * y * y))))
    y = y * _sigmoid(_dot(y, gw_ref[...]) + gb_ref[...])
    ob_ref[0] = y * z_ref[0]

    @pl.when(n == nc - 1)
    def _():
        hr_out[0] = hr_scr[...]
        hi_out[0] = hi_scr[...]


def _s5(proj3, h0r, h0i, tabs, bbd_re, bbd_im, cbd_re, cbd_im, d_row, gw, gb_row, L):
    b, t, _ = proj3.shape
    tt = min(S5_ROWS, t)
    pnr, pni, ppr, ppi, l1r, l1i = tabs
    full = lambda a: pl.BlockSpec(a.shape, lambda i, n: (0,) * a.ndim)
    state = pl.BlockSpec((1, 1, N_SSM), lambda i, n: (i, 0, 0))
    u_col = COL_BU // D_B
    return pl.pallas_call(
        functools.partial(_s5_kernel, L=L, TT=tt),
        grid=(b, t // tt),
        in_specs=[pl.BlockSpec((1, tt, D_B), lambda i, n: (i, n, u_col)),
                  pl.BlockSpec((1, tt, D_B), lambda i, n: (i, n, u_col + 1)),
                  state, state,
                  full(pnr), full(pni), full(ppr), full(ppi), full(l1r), full(l1i),
                  full(bbd_re), full(bbd_im), full(cbd_re), full(cbd_im), full(d_row), full(gw), full(gb_row)],
        out_specs=[pl.BlockSpec((1, tt, D_B), lambda i, n: (i, n, 0)), state, state],
        out_shape=[jax.ShapeDtypeStruct((b, t, D_B), F32),
                   jax.ShapeDtypeStruct((b, 1, N_SSM), F32),
                   jax.ShapeDtypeStruct((b, 1, N_SSM), F32)],
        scratch_shapes=[pltpu.VMEM((1, N_SSM), F32), pltpu.VMEM((1, N_SSM), F32),
                        pltpu.VMEM((tt, N_SSM), BF16), pltpu.VMEM((tt, N_SSM), BF16)],
        compiler_params=pltpu.CompilerParams(dimension_semantics=("parallel", "arbitrary"),
                                             vmem_limit_bytes=VMEM_LIMIT),
        name="s5",
    )(proj3, proj3, h0r, h0i, pnr, pni, ppr, ppi, l1r, l1i, bbd_re, bbd_im, cbd_re, cbd_im, d_row, gw, gb_row)


def _ml_kernel(q_ref, k_ref, v_ref, o_ref, z_ref, sm_ref, c0_ref, n0_ref, m0_ref, ib_ref, fb_ref, nrm_ref,
               oc_ref, cout_ref, nout_ref, mout_ref, c_scr, n_scr, m_scr, *, L, BB):
    n = pl.program_id(1)
    nc = pl.num_programs(1)

    @pl.when(n == 0)
    def _():
        for r in range(BB):
            c_scr[r * ML_HEADS:(r + 1) * ML_HEADS] = c0_ref[r]
        n_scr[...] = n0_ref[...]
        m_scr[...] = m0_ref[...]

    row, col, incl, _ = _tri_masks(L)
    tri = incl.astype(F32)
    lane = lax.broadcasted_iota(jnp.int32, (1, LANES), 1)
    pairs = [(r, h) for r in range(BB) for h in range(ML_HEADS)]
    sl = lambda h: slice(h * ML_DH, (h + 1) * ML_DH)

    b_l, brow_l, icol_l, irow_l, mprev_l, nvec_l = [], [], [], [], [], []
    for r in range(BB):
        sm = sm_ref[r]
        ipre_all = sm + ib_ref[...]
        xf = sm + fb_ref[...]
        logf_all = jnp.minimum(xf, 0.0) - jnp.log(1.0 + jnp.exp(-jnp.abs(xf)))
        b_all = _dot_exact(tri, logf_all)
        b_t = _transpose_exact(b_all)
        i_t = _transpose_exact(ipre_all)
        m_all = m_scr[r]
        n_all = n_scr[r]
        for h in range(ML_HEADS):
            b_l.append(b_all[:, SM_F + h:SM_F + h + 1])
            brow_l.append(b_t[SM_F + h:SM_F + h + 1, :])
            icol_l.append(ipre_all[:, SM_I + h:SM_I + h + 1])
            irow_l.append(i_t[SM_I + h:SM_I + h + 1, :])
            mprev_l.append(m_all[:, h:h + 1])
            nvec_l.append(n_all[h:h + 1, :])

    q = jnp.stack([q_ref[r, :, sl(h)] for r, h in pairs])
    k = jnp.stack([k_ref[r, :, sl(h)] for r, h in pairs])
    v = jnp.stack([v_ref[r, :, sl(h)] for r, h in pairs])
    b = jnp.stack(b_l)
    b_row = jnp.stack(brow_l)
    i_col = jnp.stack(icol_l)
    i_row = jnp.stack(irow_l)
    m_prev = jnp.stack(mprev_l)
    nvec = jnp.stack(nvec_l)
    logw = jnp.where(incl, b - b_row + i_row, -jnp.inf)
    lwm = jnp.max(logw, axis=-1, keepdims=True)
    m_t = jnp.maximum(b + m_prev, lwm)
    w_state = jnp.exp(b + m_prev - m_t)
    w_intra = jnp.exp(logw - m_t) * _bdot_nt(q, k)
    c = c_scr[...]
    num = w_state * _bdot(q, c) + _bdot(w_intra, v)
    den = w_state * jnp.sum(q * nvec, axis=-1, keepdims=True) + jnp.sum(w_intra, axis=-1, keepdims=True)
    hh = num / jnp.maximum(jnp.abs(den), jnp.exp(-m_t))
    m_new = m_t[:, L - 1:L, :]
    b_last = b[:, L - 1:L, :]
    w_keep = jnp.exp(b_last + m_prev - m_new)
    w_end = jnp.exp(b_last - b + i_col - m_new)
    kw = k * w_end
    c_scr[...] = w_keep * c + _bdot_tn(kw, v)
    n_new = w_keep * nvec + jnp.sum(kw, axis=1, keepdims=True)
    for r in range(BB):
        m_next = m_scr[r]
        for h in range(ML_HEADS):
            idx = r * ML_HEADS + h
            n_scr[r, h:h + 1, :] = n_new[idx]
            m_next = jnp.where(lane == h, m_new[idx], m_next)
            gated = o_ref[r, :, sl(h)] * hh[idx]
            oc_ref[r, :, sl(h)] = _rms(gated, nrm_ref[...]) * z_ref[r, :, sl(h)]
        m_scr[r] = m_next

    @pl.when(n == nc - 1)
    def _():
        for r in range(BB):
            cout_ref[r] = c_scr[r * ML_HEADS:(r + 1) * ML_HEADS]
        nout_ref[...] = n_scr[...]
        mout_ref[...] = m_scr[...]


def _ml(proj3, c0, n0, m0p, ib_row, fb_row, nrm_row, L):
    b, t, _ = proj3.shape
    bb = ML_ROWS
    c_col = COL_CQ // D_C
    col = lambda c: (lambda i, n: (i, n, c))
    fixed2 = lambda i, n: (0, 0)
    cspec = pl.BlockSpec((bb, ML_HEADS, ML_DH, ML_DH), lambda i, n: (i, 0, 0, 0))
    nspec = pl.BlockSpec((bb, ML_HEADS, ML_DH), lambda i, n: (i, 0, 0))
    mspec = pl.BlockSpec((bb, 1, LANES), lambda i, n: (i, 0, 0))
    return pl.pallas_call(
        functools.partial(_ml_kernel, L=L, BB=bb),
        grid=(b // bb, t // L),
        in_specs=[pl.BlockSpec((bb, L, D_C), col(c_col + j)) for j in range(5)]
        + [pl.BlockSpec((bb, L, LANES), col(COL_SMALL // LANES)),
           cspec, nspec, mspec,
           pl.BlockSpec((1, LANES), fixed2), pl.BlockSpec((1, LANES), fixed2), pl.BlockSpec((1, ML_DH), fixed2)],
        out_specs=[pl.BlockSpec((bb, L, D_C), lambda i, n: (i, n, 0)), cspec, nspec, mspec],
        out_shape=[jax.ShapeDtypeStruct((b, t, D_C), F32),
                   jax.ShapeDtypeStruct((b, ML_HEADS, ML_DH, ML_DH), F32),
                   jax.ShapeDtypeStruct((b, ML_HEADS, ML_DH), F32),
                   jax.ShapeDtypeStruct((b, 1, LANES), F32)],
        scratch_shapes=[pltpu.VMEM((bb * ML_HEADS, ML_DH, ML_DH), F32),
                        pltpu.VMEM((bb, ML_HEADS, ML_DH), F32),
                        pltpu.VMEM((bb, 1, LANES), F32)],
        compiler_params=pltpu.CompilerParams(dimension_semantics=("parallel", "arbitrary"),
                                             vmem_limit_bytes=VMEM_LIMIT),
        name="ml",
    )(proj3, proj3, proj3, proj3, proj3, proj3, c0, n0, m0p, ib_row, fb_row, nrm_row)


def _out_kernel(oa_ref, ob_ref, oc_ref, wa_ref, wb_ref, wc_ref, x_ref, g_ref, y_ref):
    acc = jnp.dot(oa_ref[...].astype(BF16), wa_ref[...], preferred_element_type=F32)
    acc = acc + jnp.dot(ob_ref[...].astype(BF16), wb_ref[...], preferred_element_type=F32)
    acc = acc + jnp.dot(oc_ref[...].astype(BF16), wc_ref[...], preferred_element_type=F32)
    y_ref[...] = x_ref[...] + _rms(acc, g_ref[...])


def _outproj(oa, ob, oc, wa, wb, wc, x2, g_row, tm):
    n = x2.shape[0]
    rows = lambda w: pl.BlockSpec((tm, w), lambda i: (i, 0))
    full = lambda a: pl.BlockSpec(a.shape, lambda i: (0, 0))
    return pl.pallas_call(
        _out_kernel,
        grid=(n // tm,),
        in_specs=[rows(D_A), rows(D_B), rows(D_C), full(wa), full(wb), full(wc), rows(D_MODEL), full(g_row)],
        out_specs=rows(D_MODEL),
        out_shape=jax.ShapeDtypeStruct((n, D_MODEL), F32),
        compiler_params=pltpu.CompilerParams(dimension_semantics=("parallel",),
                                             vmem_limit_bytes=VMEM_LIMIT),
        name="outproj",
    )(oa, ob, oc, wa, wb, wc, x2, g_row)


def _pad_lanes(v, offset):
    return jnp.zeros((1, LANES), F32).at[0, offset:offset + v.shape[0]].set(v.astype(F32))


def _block_diag_halves(blocks):
    g, r, c = blocks.shape
    gh = g // S5_HALVES
    eye = jnp.eye(gh, dtype=blocks.dtype)
    bd = jnp.einsum("agrc,gh->agrhc", blocks.reshape(S5_HALVES, gh, r, c), eye)
    return bd.reshape(S5_HALVES, gh * r, gh * c).astype(BF16)


def _prep_layer_weights(w):
    (norm_pre, norm_post, w_in, dn_conv_w, dn_A_log, dn_dt_bias, dn_norm, ssm_lam_re, ssm_lam_im, ssm_log_dt,
     ssm_B_re, ssm_B_im, ssm_C_re, ssm_C_im, ssm_D, ssm_glu_w, ssm_glu_b, ml_i_bias, ml_f_bias, ml_norm, w_out) = w
    o = 0
    seg = {}
    for name, size in (("qkv", 3 * D_A), ("az", D_A), ("beta", DN_HEADS), ("alpha", DN_HEADS), ("bu", D_B),
                       ("bz", D_B), ("cq", D_C), ("ck", D_C), ("cv", D_C), ("co", D_C), ("cz", D_C),
                       ("ci", ML_HEADS), ("cf", ML_HEADS)):
        seg[name] = w_in[:, o:o + size]
        o += size
    n_small = 2 * DN_HEADS + 2 * ML_HEADS
    w_re = jnp.concatenate(
        [seg[k] for k in ("qkv", "az", "bu", "bz", "cq", "ck", "cv", "co", "cz", "beta", "alpha", "ci", "cf")]
        + [jnp.zeros((D_MODEL, LANES - n_small), w_in.dtype)], axis=1).astype(BF16)
    cbd = lambda c: _block_diag_halves(jnp.swapaxes(c.astype(F32), 1, 2))
    return dict(
        norm_pre=norm_pre.reshape(1, D_MODEL).astype(F32), norm_post=norm_post.reshape(1, D_MODEL).astype(F32),
        w_in=w_re, conv_w=dn_conv_w.astype(F32),
        alog_row=_pad_lanes(dn_A_log, SM_ALPHA), dtb_row=_pad_lanes(dn_dt_bias, SM_ALPHA),
        dn_norm=dn_norm.reshape(1, DN_DV).astype(F32),
        lam_re=ssm_lam_re.astype(F32), lam_im=ssm_lam_im.astype(F32), log_dt=ssm_log_dt.astype(F32),
        b_re=ssm_B_re.astype(F32), b_im=ssm_B_im.astype(F32), cbd_re=cbd(ssm_C_re), cbd_im=cbd(ssm_C_im),
        d_row=ssm_D.reshape(1, D_B).astype(F32), glu_w=ssm_glu_w.astype(BF16),
        glu_b=ssm_glu_b.reshape(1, D_B).astype(F32),
        ib_row=_pad_lanes(ml_i_bias, SM_I), fb_row=_pad_lanes(ml_f_bias, SM_F),
        ml_norm=ml_norm.reshape(1, ML_DH).astype(F32),
        wo_a=w_out[:D_A].astype(BF16), wo_b=w_out[D_A:D_A + D_B].astype(BF16), wo_c=w_out[D_A + D_B:].astype(BF16))


def _s5_tables(p, L):
    pnr, pni, ppr, ppi, l1r, l1i, bbr, bbi = _s5prep(p["lam_re"], p["lam_im"], p["log_dt"], p["b_re"], p["b_im"], L)
    bbd = lambda bb: _block_diag_halves(jnp.swapaxes(bb.reshape(SSM_GROUPS, SSM_STATE, SSM_GROUP), 1, 2))
    return (pnr, pni, ppr, ppi, l1r, l1i), bbd(bbr), bbd(bbi)


def _layer(x, state, p, s5tab, L):
    conv0, s0, h0r, h0i, c0, n0, m0 = state
    b, t, _ = x.shape
    x2 = x.reshape(b * t, D_MODEL)
    conv0p = jnp.pad(conv0.astype(F32), ((0, 0), (SUBLANES - (CONV_W - 1), 0), (0, 0)))
    proj2, conv_tail = _proj(x2, p["norm_pre"], p["w_in"], p["conv_w"], conv0p, min(PROJ_ROWS, t), t)
    proj3 = proj2.reshape(b, t, DP)

    oa, s_new = _dn(proj3, s0.astype(F32), p["alog_row"], p["dtb_row"], p["dn_norm"], L)
    tabs, bbd_re, bbd_im = s5tab
    ob, hr_new, hi_new = _s5(proj3, h0r.reshape(b, 1, N_SSM).astype(F32), h0i.reshape(b, 1, N_SSM).astype(F32),
                             tabs, bbd_re, bbd_im, p["cbd_re"], p["cbd_im"], p["d_row"], p["glu_w"], p["glu_b"], L)
    m0p = jnp.pad(m0.astype(F32), ((0, 0), (0, LANES - ML_HEADS))).reshape(b, 1, LANES)
    oc, c_new, n_new, m_new = _ml(proj3, c0.astype(F32), n0.astype(F32), m0p, p["ib_row"], p["fb_row"],
                                  p["ml_norm"], L)
    y2 = _outproj(oa.reshape(b * t, D_A), ob.reshape(b * t, D_B), oc.reshape(b * t, D_C),
                  p["wo_a"], p["wo_b"], p["wo_c"], x2, p["norm_post"], min(OUT_ROWS, b * t))
    new_state = (conv_tail[:, SUBLANES - (CONV_W - 1):, :], s_new,
                 hr_new.reshape(b, SSM_GROUPS, SSM_STATE), hi_new.reshape(b, SSM_GROUPS, SSM_STATE),
                 c_new, n_new, m_new[:, 0, :ML_HEADS])
    return y2.reshape(b, t, D_MODEL), new_state


def kernel(x_prompt, x_sample, state_dn_conv, state_dn_S, state_ssm_re, state_ssm_im, state_ml_C, state_ml_n, state_ml_m, norm_pre, norm_post, w_in, dn_conv_w, dn_A_log, dn_dt_bias, dn_norm, ssm_lam_re, ssm_lam_im, ssm_log_dt, ssm_B_re, ssm_B_im, ssm_C_re, ssm_C_im, ssm_D, ssm_glu_w, ssm_glu_b, ml_i_bias, ml_f_bias, ml_norm, w_out):
    bp, tp, _ = x_prompt.shape
    bs, ts, _ = x_sample.shape
    lp = min(CHUNK, tp)
    ls = min(CHUNK, ts)
    zero_state = (jnp.zeros((bp, CONV_W - 1, 3 * D_A), F32),
                  jnp.zeros((bp, DN_HEADS, DN_DK, DN_DV), F32),
                  jnp.zeros((bp, SSM_GROUPS, SSM_STATE), F32),
                  jnp.zeros((bp, SSM_GROUPS, SSM_STATE), F32),
                  jnp.zeros((bp, ML_HEADS, ML_DH, ML_DH), F32),
                  jnp.zeros((bp, ML_HEADS, ML_DH), F32),
                  jnp.zeros((bp, ML_HEADS), F32))
    weights = (norm_pre, norm_post, w_in, dn_conv_w, dn_A_log, dn_dt_bias, dn_norm, ssm_lam_re, ssm_lam_im,
               ssm_log_dt, ssm_B_re, ssm_B_im, ssm_C_re, ssm_C_im, ssm_D, ssm_glu_w, ssm_glu_b, ml_i_bias,
               ml_f_bias, ml_norm, w_out)
    xp, xs = x_prompt, x_sample
    new_p, new_s = [], []
    for l in range(DEPTH):
        p = _prep_layer_weights(tuple(w[l] for w in weights))
        (pnr, pni, ppr, ppi, l1r, l1i), bbd_re, bbd_im = _s5_tables(p, max(lp, ls))
        tabs = lambda L: ((pnr[:L], pni[:L], ppr[:L], ppi[:L], l1r, l1i), bbd_re, bbd_im)
        xp, sp = _layer(xp, zero_state, p, tabs(lp), lp)
        carried = (state_dn_conv[l], state_dn_S[l], state_ssm_re[l], state_ssm_im[l],
                   state_ml_C[l], state_ml_n[l], state_ml_m[l])
        xs, ss = _layer(xs, carried, p, tabs(ls), ls)
        new_p.append(sp)
        new_s.append(ss)
    stack = lambda states, i, dt: jnp.stack([s[i] for s in states], axis=0).astype(dt)
    pst = [stack(new_p, i, x_prompt.dtype) for i in range(7)]
    sst = [stack(new_s, i, x_sample.dtype) for i in range(7)]
    return (xp, xs, pst[0], pst[1], pst[2], pst[3], pst[4], pst[5], pst[6],
            sst[0], sst[1], sst[2], sst[3], sst[4], sst[5], sst[6])
```

```python
import functools
import math

import jax
import jax.numpy as jnp
from jax import lax
from jax.experimental import pallas as pl
from jax.experimental.pallas import tpu as pltpu

F32 = jnp.float32
BF16 = jnp.bfloat16
HIGHEST = lax.Precision.HIGHEST

D_MODEL = 1024
DEPTH = 2
CHUNK = 64
D_MIX = 2 * D_MODEL
D_A = D_MIX // 2
DN_DK = 128
DN_DV = 128
DN_HEADS = D_A // DN_DV
CONV_W = 4
D_B = D_MIX // 4
SSM_GROUP = 16
SSM_GROUPS = D_B // SSM_GROUP
SSM_STATE = 64
D_C = D_MIX // 4
ML_DH = 128
ML_HEADS = D_C // ML_DH
EPS = 1e-6
N_SSM = SSM_GROUPS * SSM_STATE
S5_HALVES = 2
S5_ROWS = 512
S5_COLS = 256
PROJ_ROWS = 256
OUT_ROWS = 1024
DN_ROWS = 4
ML_ROWS = 4

LANES = 128
SUBLANES = 8

COL_Q, COL_K, COL_V = 0, D_A, 2 * D_A
COL_AZ = 3 * D_A
COL_BU = COL_AZ + D_A
COL_BZ = COL_BU + D_B
COL_CQ = COL_BZ + D_B
COL_SMALL = COL_CQ + 5 * D_C
DP = COL_SMALL + LANES
SM_BETA = 0
SM_ALPHA = DN_HEADS
SM_I = 2 * DN_HEADS
SM_F = 2 * DN_HEADS + ML_HEADS
PROJ_TILE = 512
_PROJ_TILES = tuple(
    [(COL_Q + c, PROJ_TILE, "q") for c in range(0, D_A, PROJ_TILE)]
    + [(COL_K + c, PROJ_TILE, "k") for c in range(0, D_A, PROJ_TILE)]
    + [(COL_V + c, PROJ_TILE, "v") for c in range(0, D_A, PROJ_TILE)]
    + [(COL_AZ + c, PROJ_TILE, "silu") for c in range(0, D_A, PROJ_TILE)]
    + [(COL_BU, D_B, "none"), (COL_BZ, D_B, "silu"),
       (COL_CQ, D_C, "none"), (COL_CQ + D_C, D_C, "kscale"), (COL_CQ + 2 * D_C, D_C, "none"),
       (COL_CQ + 3 * D_C, D_C, "sigmoid"), (COL_CQ + 4 * D_C, D_C, "silu"),
       (COL_SMALL, LANES, "none")])

VMEM_LIMIT = 48 * 1024 * 1024


def _dot(a, b):
    return jnp.dot(a.astype(BF16), b.astype(BF16), preferred_element_type=F32)


def _bdot(a, b):
    return lax.dot_general(a.astype(BF16), b.astype(BF16), (((2,), (1,)), ((0,), (0,))),
                           preferred_element_type=F32)


def _bdot_nt(a, b):
    return lax.dot_general(a.astype(BF16), b.astype(BF16), (((2,), (2,)), ((0,), (0,))),
                           preferred_element_type=F32)


def _bdot_tn(a, b):
    return lax.dot_general(a.astype(BF16), b.astype(BF16), (((1,), (1,)), ((0,), (0,))),
                           preferred_element_type=F32)


def _dot_exact(a, b):
    return jnp.dot(a, b, precision=HIGHEST, preferred_element_type=F32)


def _transpose_exact(x):
    n = x.shape[1]
    eye = (lax.broadcasted_iota(jnp.int32, (n, n), 0) == lax.broadcasted_iota(jnp.int32, (n, n), 1)).astype(F32)
    return lax.dot_general(eye, x, (((1,), (1,)), ((), ())), precision=HIGHEST, preferred_element_type=F32)


def _sigmoid(x):
    return 1.0 / (1.0 + jnp.exp(-x))


def _silu(x):
    return x * _sigmoid(x)


def _softplus(x):
    return jnp.maximum(x, 0.0) + jnp.log(1.0 + jnp.exp(-jnp.abs(x)))


def _tri_masks(L):
    row = lax.broadcasted_iota(jnp.int32, (L, L), 0)
    col = lax.broadcasted_iota(jnp.int32, (L, L), 1)
    return row, col, row >= col, row > col


def _rms(x, g):
    return x * lax.rsqrt(jnp.mean(x * x, axis=-1, keepdims=True) + EPS) * g


def _proj_kernel(x_ref, g_ref, w_ref, cw_ref, conv0_ref, o_ref, ctail_ref, carry_scr, *, tm, tiles_per_seq):
    @pl.when(pl.program_id(0) % tiles_per_seq == 0)
    def _():
        carry_scr[...] = conv0_ref[0]

    h = _rms(x_ref[...], g_ref[...]).astype(BF16)
    for c0, width, kind in _PROJ_TILES:
        cols = slice(c0, c0 + width)
        acc = jnp.dot(h, w_ref[:, cols], preferred_element_type=F32)
        if kind in ("q", "k", "v"):
            ext = jnp.concatenate([carry_scr[:, cols], acc], axis=0)
            tail = acc[tm - SUBLANES:tm, :]
            carry_scr[:, cols] = tail
            ctail_ref[0, :, cols] = tail
            y = acc * cw_ref[CONV_W - 1:CONV_W, cols]
            for d in range(1, CONV_W):
                shifted = pltpu.roll(ext, d, axis=0)[SUBLANES:SUBLANES + tm, :]
                y = y + shifted * cw_ref[CONV_W - 1 - d:CONV_W - d, cols]
            y = _silu(y)
            if kind != "v":
                parts = []
                for s in range(width // DN_DK):
                    yh = y[:, s * DN_DK:(s + 1) * DN_DK]
                    yh = yh * lax.rsqrt(jnp.sum(yh * yh, axis=-1, keepdims=True) + EPS)
                    parts.append(yh * (DN_DK ** -0.5) if kind == "q" else yh)
                y = jnp.concatenate(parts, axis=1)
            o_ref[:, cols] = y
        elif kind == "silu":
            o_ref[:, cols] = _silu(acc)
        elif kind == "sigmoid":
            o_ref[:, cols] = _sigmoid(acc)
        elif kind == "kscale":
            o_ref[:, cols] = acc * (ML_DH ** -0.5)
        else:
            o_ref[:, cols] = acc


def _proj(x2, g_row, w_bf, cw, conv0p, tm, seq_len):
    n = x2.shape[0]
    nseq = conv0p.shape[0]
    tps = seq_len // tm
    return pl.pallas_call(
        functools.partial(_proj_kernel, tm=tm, tiles_per_seq=tps),
        grid=(n // tm,),
        in_specs=[pl.BlockSpec((tm, D_MODEL), lambda i: (i, 0)),
                  pl.BlockSpec((1, D_MODEL), lambda i: (0, 0)),
                  pl.BlockSpec((D_MODEL, DP), lambda i: (0, 0), pipeline_mode=pl.Buffered(1)),
                  pl.BlockSpec((CONV_W, 3 * D_A), lambda i: (0, 0)),
                  pl.BlockSpec((1, SUBLANES, 3 * D_A), lambda i: (i // tps, 0, 0))],
        out_specs=[pl.BlockSpec((tm, DP), lambda i: (i, 0)),
                   pl.BlockSpec((1, SUBLANES, 3 * D_A), lambda i: (i // tps, 0, 0))],
        out_shape=[jax.ShapeDtypeStruct((n, DP), F32),
                   jax.ShapeDtypeStruct((nseq, SUBLANES, 3 * D_A), F32)],
        scratch_shapes=[pltpu.VMEM((SUBLANES, 3 * D_A), F32)],
        compiler_params=pltpu.CompilerParams(dimension_semantics=("arbitrary",),
                                             vmem_limit_bytes=VMEM_LIMIT),
        name="proj",
    )(x2, g_row, w_bf, cw, conv0p)


def _unit_lower_inverse(a_strict, row, col, L):
    t = None
    s = 1
    while s < L:
        sh = s.bit_length() - 1
        m = ((row >> (sh + 1)) == (col >> (sh + 1))) & (((row >> sh) & 1) == 1) & (((col >> sh) & 1) == 0)
        a_off = jnp.where(m, a_strict, 0.0)
        if t is None:
            t = jnp.where(row == col, 1.0, 0.0) - a_off
        else:
            t = t - _bdot(t, _bdot(a_off, t))
        s *= 2
    return t


def _dn_kernel(q_ref, k_ref, v_ref, z_ref, sm_ref, s0_ref, alog_ref, dtb_ref, nrm_ref,
               oa_ref, sout_ref, s_scr, *, L, BB):
    n = pl.program_id(1)
    nc = pl.num_programs(1)

    @pl.when(n == 0)
    def _():
        for r in range(BB):
            s_scr[r * DN_HEADS:(r + 1) * DN_HEADS] = s0_ref[r]

    row, col, incl, strict = _tri_masks(L)
    tri = incl.astype(F32)

    beta_l, g_l, grow_l = [], [], []
    for r in range(BB):
        sm = sm_ref[r]
        beta_all = _sigmoid(sm)
        la_all = -jnp.exp(alog_ref[...]) * _softplus(sm + dtb_ref[...])
        g_all = _dot_exact(tri, la_all)
        g_t = _transpose_exact(g_all)
        for h in range(DN_HEADS):
            beta_l.append(beta_all[:, SM_BETA + h:SM_BETA + h + 1])
            g_l.append(g_all[:, SM_ALPHA + h:SM_ALPHA + h + 1])
            grow_l.append(g_t[SM_ALPHA + h:SM_ALPHA + h + 1, :])

    pairs = [(r, h) for r in range(BB) for h in range(DN_HEADS)]
    hs = lambda h: slice(h * DN_DK, (h + 1) * DN_DK)
    q = jnp.stack([q_ref[r, :, hs(h)] for r, h in pairs])
    k = jnp.stack([k_ref[r, :, hs(h)] for r, h in pairs])
    v = jnp.stack([v_ref[r, :, hs(h)] for r, h in pairs])
    beta = jnp.stack(beta_l)
    g = jnp.stack(g_l)
    g_row = jnp.stack(grow_l)
    g_last = g[:, L - 1:L, :]
    eg = jnp.exp(g)
    decay = jnp.where(incl, jnp.exp(jnp.where(incl, g - g_row, 0.0)), 0.0)
    kq_k = _bdot_nt(jnp.concatenate([k, q], axis=1), k)
    a_mat = jnp.where(strict, beta * kq_k[:, :L, :] * decay, 0.0)
    attn = kq_k[:, L:, :] * decay
    t_inv = _unit_lower_inverse(a_mat, row, col, L)
    rhs = jnp.concatenate([k * (beta * eg), v * beta], axis=-1)
    wu = _bdot(t_inv, rhs)
    w = wu[:, :, :DN_DK]
    u = wu[:, :, DN_DK:]
    s = s_scr[...]
    wq_s = _bdot(jnp.concatenate([w, q * eg], axis=1), s)
    delta = u - wq_s[:, :L, :]
    o = wq_s[:, L:, :] + _bdot(attn, delta)
    s_scr[...] = jnp.exp(g_last) * s + _bdot_tn(k * jnp.exp(g_last - g), delta)
    o = _rms(o, nrm_ref[...])
    for idx, (r, h) in enumerate(pairs):
        oa_ref[r, :, hs(h)] = (o[idx] * z_ref[r, :, hs(h)]).astype(BF16)

    @pl.when(n == nc - 1)
    def _():
        for r in range(BB):
            sout_ref[r] = s_scr[r * DN_HEADS:(r + 1) * DN_HEADS]


def _dn(proj3, s0, alog_row, dtb_row, nrm_row, L):
    b, t, _ = proj3.shape
    bb = DN_ROWS
    col = lambda c: (lambda i, n: (i, n, c))
    fixed2 = lambda i, n: (0, 0)
    sspec = pl.BlockSpec((bb, DN_HEADS, DN_DK, DN_DV), lambda i, n: (i, 0, 0, 0))
    return pl.pallas_call(
        functools.partial(_dn_kernel, L=L, BB=bb),
        grid=(b // bb, t // L),
        in_specs=[pl.BlockSpec((bb, L, D_A), col(COL_Q // D_A)),
                  pl.BlockSpec((bb, L, D_A), col(COL_K // D_A)),
                  pl.BlockSpec((bb, L, D_A), col(COL_V // D_A)),
                  pl.BlockSpec((bb, L, D_A), col(COL_AZ // D_A)),
                  pl.BlockSpec((bb, L, LANES), col(COL_SMALL // LANES)),
                  sspec,
                  pl.BlockSpec((1, LANES), fixed2),
                  pl.BlockSpec((1, LANES), fixed2),
                  pl.BlockSpec((1, DN_DV), fixed2)],
        out_specs=[pl.BlockSpec((bb, L, D_A), lambda i, n: (i, n, 0)), sspec],
        out_shape=[jax.ShapeDtypeStruct((b, t, D_A), BF16),
                   jax.ShapeDtypeStruct((b, DN_HEADS, DN_DK, DN_DV), F32)],
        scratch_shapes=[pltpu.VMEM((bb * DN_HEADS, DN_DK, DN_DV), F32)],
        compiler_params=pltpu.CompilerParams(dimension_semantics=("parallel", "arbitrary"),
                                             vmem_limit_bytes=VMEM_LIMIT),
        name="dn",
    )(proj3, proj3, proj3, proj3, proj3, s0, alog_row, dtb_row, nrm_row)


def _s5prep_kernel(a_ref, th_ref, lr_ref, li_ref, bre_ref, bim_ref, a16_ref, th16_ref,
                   pnr_ref, pni_ref, ppr_ref, ppi_ref, l1r_ref, l1i_ref, bbr_ref, bbi_ref, *, L):
    tcol = lax.broadcasted_iota(jnp.int32, (L, 1), 0).astype(F32)
    a = a_ref[...]
    th = th_ref[...]
    ppr_ref[...] = jnp.exp(a * tcol) * jnp.cos(th * tcol)
    ppi_ref[...] = jnp.exp(a * tcol) * jnp.sin(th * tcol)
    pnr_ref[...] = jnp.exp(-a * tcol) * jnp.cos(th * tcol)
    pni_ref[...] = -jnp.exp(-a * tcol) * jnp.sin(th * tcol)
    l1r_ref[...] = jnp.exp(a) * jnp.cos(th)
    l1i_ref[...] = jnp.exp(a) * jnp.sin(th)
    a16 = a16_ref[...]
    th16 = th16_ref[...]
    lr = lr_ref[...]
    li = li_ref[...]
    nr = jnp.exp(a16) * jnp.cos(th16) - 1.0
    ni = jnp.exp(a16) * jnp.sin(th16)
    den = lr * lr + li * li
    cr = (nr * lr + ni * li) / den
    ci = (ni * lr - nr * li) / den
    bre = bre_ref[...]
    bim = bim_ref[...]
    bbr_ref[...] = cr * bre - ci * bim
    bbi_ref[...] = cr * bim + ci * bre


def _s5prep(lam_re, lam_im, log_dt, b_re, b_im, L):
    dt = jnp.exp(log_dt)[:, None]
    a_row = (lam_re * dt).reshape(1, N_SSM)
    th_row = (lam_im * dt).reshape(1, N_SSM)
    rep = lambda x: jnp.broadcast_to(x.reshape(N_SSM, 1), (N_SSM, SSM_GROUP))
    outs = pl.pallas_call(
        functools.partial(_s5prep_kernel, L=L),
        out_shape=[jax.ShapeDtypeStruct((L, N_SSM), F32)] * 4
        + [jax.ShapeDtypeStruct((1, N_SSM), F32)] * 2
        + [jax.ShapeDtypeStruct((N_SSM, SSM_GROUP), F32)] * 2,
        name="s5prep",
    )(a_row, th_row, rep(lam_re), rep(lam_im), b_re.reshape(N_SSM, SSM_GROUP), b_im.reshape(N_SSM, SSM_GROUP),
      rep(lam_re * dt), rep(lam_im * dt))
    return outs


def _s5_kernel(u_ref, z_ref, h0r_ref, h0i_ref, pnr_ref, pni_ref, ppr_ref, ppi_ref, l1r_ref, l1i_ref,
               bre_ref, bim_ref, cre_ref, cim_ref, d_ref, gw_ref, gb_ref,
               ob_ref, hr_out, hi_out, hr_scr, hi_scr, hbr_scr, hbi_scr, *, L, TT):
    n = pl.program_id(1)
    nc = pl.num_programs(1)

    @pl.when(n == 0)
    def _():
        hr_scr[...] = h0r_ref[0]
        hi_scr[...] = h0i_ref[0]

    _, _, incl, _ = _tri_masks(L)
    tri = incl.astype(BF16)
    u = u_ref[0]
    ub = u.astype(BF16)
    halves = range(S5_HALVES)
    kh = D_B // S5_HALVES
    nh = N_SSM // S5_HALVES
    ncb = N_SSM // S5_COLS
    bus = []
    for cb in range(ncb):
        a = (cb * S5_COLS) // nh
        lc = cb * S5_COLS - a * nh
        ua = ub[:, a * kh:(a + 1) * kh]
        bus.append((jnp.dot(ua, bre_ref[a, :, lc:lc + S5_COLS], preferred_element_type=F32),
                    jnp.dot(ua, bim_ref[a, :, lc:lc + S5_COLS], preferred_element_type=F32)))
    for cb in range(ncb):
        cs = slice(cb * S5_COLS, (cb + 1) * S5_COLS)
        bur, bui = bus[cb]
        l1r = l1r_ref[:, cs]
        l1i = l1i_ref[:, cs]
        hr0 = hr_scr[:, cs]
        hi0 = hi_scr[:, cs]
        for j in range(TT // L):
            rows = slice(j * L, (j + 1) * L)
            pnr = pnr_ref[:, cs]
            pni = pni_ref[:, cs]
            vr = pnr * bur[rows] - pni * bui[rows]
            vi = pnr * bui[rows] + pni * bur[rows]
            cr = _dot(tri, vr) + (l1r * hr0 - l1i * hi0)
            ci = _dot(tri, vi) + (l1r * hi0 + l1i * hr0)
            ppr = ppr_ref[:, cs]
            ppi = ppi_ref[:, cs]
            hr = ppr * cr - ppi * ci
            hi = ppr * ci + ppi * cr
            hr0 = hr[L - 1:L, :]
            hi0 = hi[L - 1:L, :]
            hbr_scr[rows, cs] = hr.astype(BF16)
            hbi_scr[rows, cs] = hi.astype(BF16)
        hr_scr[:, cs] = hr0
        hi_scr[:, cs] = hi0
    y = jnp.concatenate([jnp.dot(hbr_scr[:, a * nh:(a + 1) * nh], cre_ref[a], preferred_element_type=F32)
                         - jnp.dot(hbi_scr[:, a * nh:(a + 1) * nh], cim_ref[a], preferred_element_type=F32)
                         for a in halves], axis=1) + d_ref[...] * u
    c0 = math.sqrt(2.0 / math.pi)
    y = 0.5 * y * (1.0 + jnp.tanh(c0 * (y + 0.044715 * (y * y * y))))
    y = y * _sigmoid(_dot(y, gw_ref[...]) + gb_ref[...])
    ob_ref[0] = (y * z_ref[0]).astype(BF16)

    @pl.when(n == nc - 1)
    def _():
        hr_out[0] = hr_scr[...]
        hi_out[0] = hi_scr[...]


def _s5(proj3, h0r, h0i, tabs, bbd_re, bbd_im, cbd_re, cbd_im, d_row, gw, gb_row, L):
    b, t, _ = proj3.shape
    tt = min(S5_ROWS, t)
    pnr, pni, ppr, ppi, l1r, l1i = tabs
    full = lambda a: pl.BlockSpec(a.shape, lambda i, n: (0,) * a.ndim)
    state = pl.BlockSpec((1, 1, N_SSM), lambda i, n: (i, 0, 0))
    u_col = COL_BU // D_B
    return pl.pallas_call(
        functools.partial(_s5_kernel, L=L, TT=tt),
        grid=(b, t // tt),
        in_specs=[pl.BlockSpec((1, tt, D_B), lambda i, n: (i, n, u_col)),
                  pl.BlockSpec((1, tt, D_B), lambda i, n: (i, n, u_col + 1)),
                  state, state,
                  full(pnr), full(pni), full(ppr), full(ppi), full(l1r), full(l1i),
                  full(bbd_re), full(bbd_im), full(cbd_re), full(cbd_im), full(d_row), full(gw), full(gb_row)],
        out_specs=[pl.BlockSpec((1, tt, D_B), lambda i, n: (i, n, 0)), state, state],
        out_shape=[jax.ShapeDtypeStruct((b, t, D_B), BF16),
                   jax.ShapeDtypeStruct((b, 1, N_SSM), F32),
                   jax.ShapeDtypeStruct((b, 1, N_SSM), F32)],
        scratch_shapes=[pltpu.VMEM((1, N_SSM), F32), pltpu.VMEM((1, N_SSM), F32),
                        pltpu.VMEM((tt, N_SSM), BF16), pltpu.VMEM((tt, N_SSM), BF16)],
        compiler_params=pltpu.CompilerParams(dimension_semantics=("parallel", "arbitrary"),
                                             vmem_limit_bytes=VMEM_LIMIT),
        name="s5",
    )(proj3, proj3, h0r, h0i, pnr, pni, ppr, ppi, l1r, l1i, bbd_re, bbd_im, cbd_re, cbd_im, d_row, gw, gb_row)


def _ml_kernel(q_ref, k_ref, v_ref, o_ref, z_ref, sm_ref, c0_ref, n0_ref, m0_ref, ib_ref, fb_ref, nrm_ref,
               oc_ref, cout_ref, nout_ref, mout_ref, c_scr, n_scr, m_scr, *, L, BB):
    n = pl.program_id(1)
    nc = pl.num_programs(1)

    @pl.when(n == 0)
    def _():
        for r in range(BB):
            c_scr[r * ML_HEADS:(r + 1) * ML_HEADS] = c0_ref[r]
        n_scr[...] = n0_ref[...]
        m_scr[...] = m0_ref[...]

    row, col, incl, _ = _tri_masks(L)
    tri = incl.astype(F32)
    lane = lax.broadcasted_iota(jnp.int32, (1, LANES), 1)
    pairs = [(r, h) for r in range(BB) for h in range(ML_HEADS)]
    sl = lambda h: slice(h * ML_DH, (h + 1) * ML_DH)

    b_l, brow_l, icol_l, irow_l, mprev_l, nvec_l = [], [], [], [], [], []
    for r in range(BB):
        sm = sm_ref[r]
        ipre_all = sm + ib_ref[...]
        xf = sm + fb_ref[...]
        logf_all = jnp.minimum(xf, 0.0) - jnp.log(1.0 + jnp.exp(-jnp.abs(xf)))
        b_all = _dot_exact(tri, logf_all)
        b_t = _transpose_exact(b_all)
        i_t = _transpose_exact(ipre_all)
        m_all = m_scr[r]
        n_all = n_scr[r]
        for h in range(ML_HEADS):
            b_l.append(b_all[:, SM_F + h:SM_F + h + 1])
            brow_l.append(b_t[SM_F + h:SM_F + h + 1, :])
            icol_l.append(ipre_all[:, SM_I + h:SM_I + h + 1])
            irow_l.append(i_t[SM_I + h:SM_I + h + 1, :])
            mprev_l.append(m_all[:, h:h + 1])
            nvec_l.append(n_all[h:h + 1, :])

    q = jnp.stack([q_ref[r, :, sl(h)] for r, h in pairs])
    k = jnp.stack([k_ref[r, :, sl(h)] for r, h in pairs])
    v = jnp.stack([v_ref[r, :, sl(h)] for r, h in pairs])
    b = jnp.stack(b_l)
    b_row = jnp.stack(brow_l)
    i_col = jnp.stack(icol_l)
    i_row = jnp.stack(irow_l)
    m_prev = jnp.stack(mprev_l)
    nvec = jnp.stack(nvec_l)
    logw = jnp.where(incl, b - b_row + i_row, -jnp.inf)
    lwm = jnp.max(logw, axis=-1, keepdims=True)
    m_t = jnp.maximum(b + m_prev, lwm)
    w_state = jnp.exp(b + m_prev - m_t)
    w_intra = jnp.exp(logw - m_t) * _bdot_nt(q, k)
    c = c_scr[...]
    num = w_state * _bdot(q, c) + _bdot(w_intra, v)
    den = w_state * jnp.sum(q * nvec, axis=-1, keepdims=True) + jnp.sum(w_intra, axis=-1, keepdims=True)
    hh = num / jnp.maximum(jnp.abs(den), jnp.exp(-m_t))
    m_new = m_t[:, L - 1:L, :]
    b_last = b[:, L - 1:L, :]
    w_keep = jnp.exp(b_last + m_prev - m_new)
    w_end = jnp.exp(b_last - b + i_col - m_new)
    kw = k * w_end
    c_scr[...] = w_keep * c + _bdot_tn(kw, v)
    n_new = w_keep * nvec + jnp.sum(kw, axis=1, keepdims=True)
    for r in range(BB):
        m_next = m_scr[r]
        for h in range(ML_HEADS):
            idx = r * ML_HEADS + h
            n_scr[r, h:h + 1, :] = n_new[idx]
            m_next = jnp.where(lane == h, m_new[idx], m_next)
            gated = o_ref[r, :, sl(h)] * hh[idx]
            oc_ref[r, :, sl(h)] = (_rms(gated, nrm_ref[...]) * z_ref[r, :, sl(h)]).astype(BF16)
        m_scr[r] = m_next

    @pl.when(n == nc - 1)
    def _():
        for r in range(BB):
            cout_ref[r] = c_scr[r * ML_HEADS:(r + 1) * ML_HEADS]
        nout_ref[...] = n_scr[...]
        mout_ref[...] = m_scr[...]


def _ml(proj3, c0, n0, m0p, ib_row, fb_row, nrm_row, L):
    b, t, _ = proj3.shape
    bb = ML_ROWS
    c_col = COL_CQ // D_C
    col = lambda c: (lambda i, n: (i, n, c))
    fixed2 = lambda i, n: (0, 0)
    cspec = pl.BlockSpec((bb, ML_HEADS, ML_DH, ML_DH), lambda i, n: (i, 0, 0, 0))
    nspec = pl.BlockSpec((bb, ML_HEADS, ML_DH), lambda i, n: (i, 0, 0))
    mspec = pl.BlockSpec((bb, 1, LANES), lambda i, n: (i, 0, 0))
    return pl.pallas_call(
        functools.partial(_ml_kernel, L=L, BB=bb),
        grid=(b // bb, t // L),
        in_specs=[pl.BlockSpec((bb, L, D_C), col(c_col + j)) for j in range(5)]
        + [pl.BlockSpec((bb, L, LANES), col(COL_SMALL // LANES)),
           cspec, nspec, mspec,
           pl.BlockSpec((1, LANES), fixed2), pl.BlockSpec((1, LANES), fixed2), pl.BlockSpec((1, ML_DH), fixed2)],
        out_specs=[pl.BlockSpec((bb, L, D_C), lambda i, n: (i, n, 0)), cspec, nspec, mspec],
        out_shape=[jax.ShapeDtypeStruct((b, t, D_C), BF16),
                   jax.ShapeDtypeStruct((b, ML_HEADS, ML_DH, ML_DH), F32),
                   jax.ShapeDtypeStruct((b, ML_HEADS, ML_DH), F32),
                   jax.ShapeDtypeStruct((b, 1, LANES), F32)],
        scratch_shapes=[pltpu.VMEM((bb * ML_HEADS, ML_DH, ML_DH), F32),
                        pltpu.VMEM((bb, ML_HEADS, ML_DH), F32),
                        pltpu.VMEM((bb, 1, LANES), F32)],
        compiler_params=pltpu.CompilerParams(dimension_semantics=("parallel", "arbitrary"),
                                             vmem_limit_bytes=VMEM_LIMIT),
        name="ml",
    )(proj3, proj3, proj3, proj3, proj3, proj3, c0, n0, m0p, ib_row, fb_row, nrm_row)


def _out_kernel(oa_ref, ob_ref, oc_ref, w_ref, x_ref, g_ref, y_ref):
    mix = jnp.concatenate([oa_ref[...], ob_ref[...], oc_ref[...]], axis=1)
    acc = jnp.dot(mix, w_ref[...], preferred_element_type=F32)
    y_ref[...] = x_ref[...] + _rms(acc, g_ref[...])


def _outproj(oa, ob, oc, w_bf, x2, g_row, tm):
    n = x2.shape[0]
    rows = lambda w: pl.BlockSpec((tm, w), lambda i: (i, 0))
    full = lambda a: pl.BlockSpec(a.shape, lambda i: (0, 0))
    return pl.pallas_call(
        _out_kernel,
        grid=(n // tm,),
        in_specs=[rows(D_A), rows(D_B), rows(D_C), full(w_bf), rows(D_MODEL), full(g_row)],
        out_specs=rows(D_MODEL),
        out_shape=jax.ShapeDtypeStruct((n, D_MODEL), F32),
        compiler_params=pltpu.CompilerParams(dimension_semantics=("parallel",),
                                             vmem_limit_bytes=VMEM_LIMIT),
        name="outproj",
    )(oa, ob, oc, w_bf, x2, g_row)


def _pad_lanes(v, offset):
    return jnp.zeros((1, LANES), F32).at[0, offset:offset + v.shape[0]].set(v.astype(F32))


def _block_diag_halves(blocks):
    g, r, c = blocks.shape
    gh = g // S5_HALVES
    eye = jnp.eye(gh, dtype=blocks.dtype)
    bd = jnp.einsum("agrc,gh->agrhc", blocks.reshape(S5_HALVES, gh, r, c), eye)
    return bd.reshape(S5_HALVES, gh * r, gh * c).astype(BF16)


def _prep_layer_weights(w):
    (norm_pre, norm_post, w_in, dn_conv_w, dn_A_log, dn_dt_bias, dn_norm, ssm_lam_re, ssm_lam_im, ssm_log_dt,
     ssm_B_re, ssm_B_im, ssm_C_re, ssm_C_im, ssm_D, ssm_glu_w, ssm_glu_b, ml_i_bias, ml_f_bias, ml_norm, w_out) = w
    o = 0
    seg = {}
    for name, size in (("qkv", 3 * D_A), ("az", D_A), ("beta", DN_HEADS), ("alpha", DN_HEADS), ("bu", D_B),
                       ("bz", D_B), ("cq", D_C), ("ck", D_C), ("cv", D_C), ("co", D_C), ("cz", D_C),
                       ("ci", ML_HEADS), ("cf", ML_HEADS)):
        seg[name] = w_in[:, o:o + size]
        o += size
    n_small = 2 * DN_HEADS + 2 * ML_HEADS
    w_re = jnp.concatenate(
        [seg[k] for k in ("qkv", "az", "bu", "bz", "cq", "ck", "cv", "co", "cz", "beta", "alpha", "ci", "cf")]
        + [jnp.zeros((D_MODEL, LANES - n_small), w_in.dtype)], axis=1).astype(BF16)
    cbd = lambda c: _block_diag_halves(jnp.swapaxes(c.astype(F32), 1, 2))
    return dict(
        norm_pre=norm_pre.reshape(1, D_MODEL).astype(F32), norm_post=norm_post.reshape(1, D_MODEL).astype(F32),
        w_in=w_re, conv_w=dn_conv_w.astype(F32),
        alog_row=_pad_lanes(dn_A_log, SM_ALPHA), dtb_row=_pad_lanes(dn_dt_bias, SM_ALPHA),
        dn_norm=dn_norm.reshape(1, DN_DV).astype(F32),
        lam_re=ssm_lam_re.astype(F32), lam_im=ssm_lam_im.astype(F32), log_dt=ssm_log_dt.astype(F32),
        b_re=ssm_B_re.astype(F32), b_im=ssm_B_im.astype(F32), cbd_re=cbd(ssm_C_re), cbd_im=cbd(ssm_C_im),
        d_row=ssm_D.reshape(1, D_B).astype(F32), glu_w=ssm_glu_w.astype(BF16),
        glu_b=ssm_glu_b.reshape(1, D_B).astype(F32),
        ib_row=_pad_lanes(ml_i_bias, SM_I), fb_row=_pad_lanes(ml_f_bias, SM_F),
        ml_norm=ml_norm.reshape(1, ML_DH).astype(F32),
        w_out=w_out.astype(BF16))


def _s5_tables(p, L):
    pnr, pni, ppr, ppi, l1r, l1i, bbr, bbi = _s5prep(p["lam_re"], p["lam_im"], p["log_dt"], p["b_re"], p["b_im"], L)
    bbd = lambda bb: _block_diag_halves(jnp.swapaxes(bb.reshape(SSM_GROUPS, SSM_STATE, SSM_GROUP), 1, 2))
    return (pnr, pni, ppr, ppi, l1r, l1i), bbd(bbr), bbd(bbi)


def _layer(x, state, p, s5tab, L):
    conv0, s0, h0r, h0i, c0, n0, m0 = state
    b, t, _ = x.shape
    x2 = x.reshape(b * t, D_MODEL)
    conv0p = jnp.pad(conv0.astype(F32), ((0, 0), (SUBLANES - (CONV_W - 1), 0), (0, 0)))
    proj2, conv_tail = _proj(x2, p["norm_pre"], p["w_in"], p["conv_w"], conv0p, min(PROJ_ROWS, t), t)
    proj3 = proj2.reshape(b, t, DP)

    oa, s_new = _dn(proj3, s0.astype(F32), p["alog_row"], p["dtb_row"], p["dn_norm"], L)
    tabs, bbd_re, bbd_im = s5tab
    ob, hr_new, hi_new = _s5(proj3, h0r.reshape(b, 1, N_SSM).astype(F32), h0i.reshape(b, 1, N_SSM).astype(F32),
                             tabs, bbd_re, bbd_im, p["cbd_re"], p["cbd_im"], p["d_row"], p["glu_w"], p["glu_b"], L)
    m0p = jnp.pad(m0.astype(F32), ((0, 0), (0, LANES - ML_HEADS))).reshape(b, 1, LANES)
    oc, c_new, n_new, m_new = _ml(proj3, c0.astype(F32), n0.astype(F32), m0p, p["ib_row"], p["fb_row"],
                                  p["ml_norm"], L)
    y2 = _outproj(oa.reshape(b * t, D_A), ob.reshape(b * t, D_B), oc.reshape(b * t, D_C),
                  p["w_out"], x2, p["norm_post"], min(OUT_ROWS, b * t))
    new_state = (conv_tail[:, SUBLANES - (CONV_W - 1):, :], s_new,
                 hr_new.reshape(b, SSM_GROUPS, SSM_STATE), hi_new.reshape(b, SSM_GROUPS, SSM_STATE),
                 c_new, n_new, m_new[:, 0, :ML_HEADS])
    return y2.reshape(b, t, D_MODEL), new_state


def kernel(x_prompt, x_sample, state_dn_conv, state_dn_S, state_ssm_re, state_ssm_im, state_ml_C, state_ml_n, state_ml_m, norm_pre, norm_post, w_in, dn_conv_w, dn_A_log, dn_dt_bias, dn_norm, ssm_lam_re, ssm_lam_im, ssm_log_dt, ssm_B_re, ssm_B_im, ssm_C_re, ssm_C_im, ssm_D, ssm_glu_w, ssm_glu_b, ml_i_bias, ml_f_bias, ml_norm, w_out):
    bp, tp, _ = x_prompt.shape
    bs, ts, _ = x_sample.shape
    lp = min(CHUNK, tp)
    ls = min(CHUNK, ts)
    zero_state = (jnp.zeros((bp, CONV_W - 1, 3 * D_A), F32),
                  jnp.zeros((bp, DN_HEADS, DN_DK, DN_DV), F32),
                  jnp.zeros((bp, SSM_GROUPS, SSM_STATE), F32),
                  jnp.zeros((bp, SSM_GROUPS, SSM_STATE), F32),
                  jnp.zeros((bp, ML_HEADS, ML_DH, ML_DH), F32),
                  jnp.zeros((bp, ML_HEADS, ML_DH), F32),
                  jnp.zeros((bp, ML_HEADS), F32))
    weights = (norm_pre, norm_post, w_in, dn_conv_w, dn_A_log, dn_dt_bias, dn_norm, ssm_lam_re, ssm_lam_im,
               ssm_log_dt, ssm_B_re, ssm_B_im, ssm_C_re, ssm_C_im, ssm_D, ssm_glu_w, ssm_glu_b, ml_i_bias,
               ml_f_bias, ml_norm, w_out)
    xp, xs = x_prompt, x_sample
    new_p, new_s = [], []
    for l in range(DEPTH):
        p = _prep_layer_weights(tuple(w[l] for w in weights))
        (pnr, pni, ppr, ppi, l1r, l1i), bbd_re, bbd_im = _s5_tables(p, max(lp, ls))
        tabs = lambda L: ((pnr[:L], pni[:L], ppr[:L], ppi[:L], l1r, l1i), bbd_re, bbd_im)
        xp, sp = _layer(xp, zero_state, p, tabs(lp), lp)
        carried = (state_dn_conv[l], state_dn_S[l], state_ssm_re[l], state_ssm_im[l],
                   state_ml_C[l], state_ml_n[l], state_ml_m[l])
        xs, ss = _layer(xs, carried, p, tabs(ls), ls)
        new_p.append(sp)
        new_s.append(ss)
    stack = lambda states, i, dt: jnp.stack([s[i] for s in states], axis=0).astype(dt)
    pst = [stack(new_p, i, x_prompt.dtype) for i in range(7)]
    sst = [stack(new_s, i, x_sample.dtype) for i in range(7)]
    return (xp, xs, pst[0], pst[1], pst[2], pst[3], pst[4], pst[5], pst[6],
            sst[0], sst[1], sst[2], sst[3], sst[4], sst[5], sst[6])
```

```python
import functools
import math

import jax
import jax.numpy as jnp
from jax import lax
from jax.experimental import pallas as pl
from jax.experimental.pallas import tpu as pltpu

F32 = jnp.float32
BF16 = jnp.bfloat16

D_MODEL = 1024
DEPTH = 2
CHUNK = 64
D_MIX = 2 * D_MODEL
D_A = D_MIX // 2
DN_DK = 128
DN_DV = 128
DN_HEADS = D_A // DN_DV
CONV_W = 4
D_B = D_MIX // 4
SSM_GROUP = 16
SSM_GROUPS = D_B // SSM_GROUP
SSM_STATE = 64
D_C = D_MIX // 4
ML_DH = 128
ML_HEADS = D_C // ML_DH
EPS = 1e-6
N_SSM = SSM_GROUPS * SSM_STATE
S5_HALVES = 2
S5_ROWS = 512
S5_COLS = 256
PROJ_ROWS = 256
OUT_ROWS = 1024
DN_ROWS = 4
ML_ROWS = 4

LANES = 128
SUBLANES = 8

COL_Q, COL_K, COL_V = 0, D_A, 2 * D_A
COL_AZ = 3 * D_A
COL_BU = COL_AZ + D_A
COL_BZ = COL_BU + D_B
COL_CQ = COL_BZ + D_B
COL_SMALL = COL_CQ + 5 * D_C
DP = COL_SMALL + LANES
SM_BETA = 0
SM_ALPHA = DN_HEADS
SM_I = 2 * DN_HEADS
SM_F = 2 * DN_HEADS + ML_HEADS
PROJ_TILE = 512
_PROJ_TILES = tuple(
    [(COL_Q + c, PROJ_TILE, "q") for c in range(0, D_A, PROJ_TILE)]
    + [(COL_K + c, PROJ_TILE, "k") for c in range(0, D_A, PROJ_TILE)]
    + [(COL_V + c, PROJ_TILE, "v") for c in range(0, D_A, PROJ_TILE)]
    + [(COL_AZ + c, PROJ_TILE, "silu") for c in range(0, D_A, PROJ_TILE)]
    + [(COL_BU, D_B, "none"), (COL_BZ, D_B, "silu"),
       (COL_CQ, D_C, "none"), (COL_CQ + D_C, D_C, "kscale"), (COL_CQ + 2 * D_C, D_C, "none"),
       (COL_CQ + 3 * D_C, D_C, "sigmoid"), (COL_CQ + 4 * D_C, D_C, "silu"),
       (COL_SMALL, LANES, "none")])

VMEM_LIMIT = 48 * 1024 * 1024


def _dot(a, b):
    return jnp.dot(a.astype(BF16), b.astype(BF16), preferred_element_type=F32)


def _bdot(a, b):
    return lax.dot_general(a.astype(BF16), b.astype(BF16), (((2,), (1,)), ((0,), (0,))),
                           preferred_element_type=F32)


def _bdot_nt(a, b):
    return lax.dot_general(a.astype(BF16), b.astype(BF16), (((2,), (2,)), ((0,), (0,))),
                           preferred_element_type=F32)


def _bdot_tn(a, b):
    return lax.dot_general(a.astype(BF16), b.astype(BF16), (((1,), (1,)), ((0,), (0,))),
                           preferred_element_type=F32)


def _cumsum_rows(x):
    n = x.shape[0]
    row = lax.broadcasted_iota(jnp.int32, x.shape, 0)
    d = 1
    while d < n:
        x = x + jnp.where(row >= d, pltpu.roll(x, d, axis=0), 0.0)
        d *= 2
    return x


def _transpose_exact(x):
    r = x.shape[0]
    if r < LANES:
        x = jnp.concatenate([x, jnp.zeros((LANES - r, LANES), x.dtype)], axis=0)
    return x.T[:, :r]


def _sigmoid(x):
    return 1.0 / (1.0 + jnp.exp(-x))


def _silu(x):
    return x * _sigmoid(x)


def _softplus(x):
    return jnp.maximum(x, 0.0) + jnp.log(1.0 + jnp.exp(-jnp.abs(x)))


def _tri_masks(L):
    row = lax.broadcasted_iota(jnp.int32, (L, L), 0)
    col = lax.broadcasted_iota(jnp.int32, (L, L), 1)
    return row, col, row >= col, row > col


def _rms(x, g):
    return x * lax.rsqrt(jnp.mean(x * x, axis=-1, keepdims=True) + EPS) * g


def _proj_kernel(x_ref, g_ref, w_ref, cw_ref, conv0_ref, o_ref, ctail_ref, carry_scr, *, tm, tiles_per_seq):
    @pl.when(pl.program_id(0) % tiles_per_seq == 0)
    def _():
        carry_scr[...] = conv0_ref[0]

    h = _rms(x_ref[...], g_ref[...]).astype(BF16)
    for c0, width, kind in _PROJ_TILES:
        cols = slice(c0, c0 + width)
        acc = jnp.dot(h, w_ref[:, cols], preferred_element_type=F32)
        if kind in ("q", "k", "v"):
            ext = jnp.concatenate([carry_scr[:, cols], acc], axis=0)
            tail = acc[tm - SUBLANES:tm, :]
            carry_scr[:, cols] = tail
            ctail_ref[0, :, cols] = tail
            y = acc * cw_ref[CONV_W - 1:CONV_W, cols]
            for d in range(1, CONV_W):
                shifted = pltpu.roll(ext, d, axis=0)[SUBLANES:SUBLANES + tm, :]
                y = y + shifted * cw_ref[CONV_W - 1 - d:CONV_W - d, cols]
            y = _silu(y)
            if kind != "v":
                parts = []
                for s in range(width // DN_DK):
                    yh = y[:, s * DN_DK:(s + 1) * DN_DK]
                    yh = yh * lax.rsqrt(jnp.sum(yh * yh, axis=-1, keepdims=True) + EPS)
                    parts.append(yh * (DN_DK ** -0.5) if kind == "q" else yh)
                y = jnp.concatenate(parts, axis=1)
            o_ref[:, cols] = y
        elif kind == "silu":
            o_ref[:, cols] = _silu(acc)
        elif kind == "sigmoid":
            o_ref[:, cols] = _sigmoid(acc)
        elif kind == "kscale":
            o_ref[:, cols] = acc * (ML_DH ** -0.5)
        else:
            o_ref[:, cols] = acc


def _proj(x2, g_row, w_bf, cw, conv0p, tm, seq_len):
    n = x2.shape[0]
    nseq = conv0p.shape[0]
    tps = seq_len // tm
    return pl.pallas_call(
        functools.partial(_proj_kernel, tm=tm, tiles_per_seq=tps),
        grid=(n // tm,),
        in_specs=[pl.BlockSpec((tm, D_MODEL), lambda i: (i, 0)),
                  pl.BlockSpec((1, D_MODEL), lambda i: (0, 0)),
                  pl.BlockSpec((D_MODEL, DP), lambda i: (0, 0), pipeline_mode=pl.Buffered(1)),
                  pl.BlockSpec((CONV_W, 3 * D_A), lambda i: (0, 0)),
                  pl.BlockSpec((1, SUBLANES, 3 * D_A), lambda i: (i // tps, 0, 0))],
        out_specs=[pl.BlockSpec((tm, DP), lambda i: (i, 0)),
                   pl.BlockSpec((1, SUBLANES, 3 * D_A), lambda i: (i // tps, 0, 0))],
        out_shape=[jax.ShapeDtypeStruct((n, DP), F32),
                   jax.ShapeDtypeStruct((nseq, SUBLANES, 3 * D_A), F32)],
        scratch_shapes=[pltpu.VMEM((SUBLANES, 3 * D_A), F32)],
        compiler_params=pltpu.CompilerParams(dimension_semantics=("arbitrary",),
                                             vmem_limit_bytes=VMEM_LIMIT),
        name="proj",
    )(x2, g_row, w_bf, cw, conv0p)


def _unit_lower_inverse(a_strict, row, col, L):
    t = None
    s = 1
    while s < L:
        sh = s.bit_length() - 1
        m = ((row >> (sh + 1)) == (col >> (sh + 1))) & (((row >> sh) & 1) == 1) & (((col >> sh) & 1) == 0)
        a_off = jnp.where(m, a_strict, 0.0)
        if t is None:
            t = jnp.where(row == col, 1.0, 0.0) - a_off
        else:
            t = t - _bdot(t, _bdot(a_off, t))
        s *= 2
    return t


def _dn_kernel(q_ref, k_ref, v_ref, z_ref, sm_ref, s0_ref, alog_ref, dtb_ref, nrm_ref,
               oa_ref, sout_ref, s_scr, *, L, BB):
    n = pl.program_id(1)
    nc = pl.num_programs(1)

    @pl.when(n == 0)
    def _():
        for r in range(BB):
            s_scr[r * DN_HEADS:(r + 1) * DN_HEADS] = s0_ref[r]

    row, col, incl, strict = _tri_masks(L)

    beta_l, g_l, grow_l = [], [], []
    for r in range(BB):
        sm = sm_ref[r]
        beta_all = _sigmoid(sm)
        la_all = -jnp.exp(alog_ref[...]) * _softplus(sm + dtb_ref[...])
        g_all = _cumsum_rows(la_all)
        g_t = _transpose_exact(g_all)
        for h in range(DN_HEADS):
            beta_l.append(beta_all[:, SM_BETA + h:SM_BETA + h + 1])
            g_l.append(g_all[:, SM_ALPHA + h:SM_ALPHA + h + 1])
            grow_l.append(g_t[SM_ALPHA + h:SM_ALPHA + h + 1, :])

    pairs = [(r, h) for r in range(BB) for h in range(DN_HEADS)]
    hs = lambda h: slice(h * DN_DK, (h + 1) * DN_DK)
    q = jnp.stack([q_ref[r, :, hs(h)] for r, h in pairs])
    k = jnp.stack([k_ref[r, :, hs(h)] for r, h in pairs])
    v = jnp.stack([v_ref[r, :, hs(h)] for r, h in pairs])
    beta = jnp.stack(beta_l)
    g = jnp.stack(g_l)
    g_row = jnp.stack(grow_l)
    g_last = g[:, L - 1:L, :]
    eg = jnp.exp(g)
    decay = jnp.where(incl, jnp.exp(jnp.where(incl, g - g_row, 0.0)), 0.0)
    kq_k = _bdot_nt(jnp.concatenate([k, q], axis=1), k)
    a_mat = jnp.where(strict, beta * kq_k[:, :L, :] * decay, 0.0)
    attn = kq_k[:, L:, :] * decay
    t_inv = _unit_lower_inverse(a_mat, row, col, L)
    rhs = jnp.concatenate([k * (beta * eg), v * beta], axis=-1)
    wu = _bdot(t_inv, rhs)
    w = wu[:, :, :DN_DK]
    u = wu[:, :, DN_DK:]
    s = s_scr[...]
    wq_s = _bdot(jnp.concatenate([w, q * eg], axis=1), s)
    delta = u - wq_s[:, :L, :]
    o = wq_s[:, L:, :] + _bdot(attn, delta)
    s_scr[...] = jnp.exp(g_last) * s + _bdot_tn(k * jnp.exp(g_last - g), delta)
    o = _rms(o, nrm_ref[...])
    for idx, (r, h) in enumerate(pairs):
        oa_ref[r, :, hs(h)] = (o[idx] * z_ref[r, :, hs(h)]).astype(BF16)

    @pl.when(n == nc - 1)
    def _():
        for r in range(BB):
            sout_ref[r] = s_scr[r * DN_HEADS:(r + 1) * DN_HEADS]


def _dn(proj3, s0, alog_row, dtb_row, nrm_row, L):
    b, t, _ = proj3.shape
    bb = DN_ROWS
    col = lambda c: (lambda i, n: (i, n, c))
    fixed2 = lambda i, n: (0, 0)
    sspec = pl.BlockSpec((bb, DN_HEADS, DN_DK, DN_DV), lambda i, n: (i, 0, 0, 0))
    return pl.pallas_call(
        functools.partial(_dn_kernel, L=L, BB=bb),
        grid=(b // bb, t // L),
        in_specs=[pl.BlockSpec((bb, L, D_A), col(COL_Q // D_A)),
                  pl.BlockSpec((bb, L, D_A), col(COL_K // D_A)),
                  pl.BlockSpec((bb, L, D_A), col(COL_V // D_A)),
                  pl.BlockSpec((bb, L, D_A), col(COL_AZ // D_A)),
                  pl.BlockSpec((bb, L, LANES), col(COL_SMALL // LANES)),
                  sspec,
                  pl.BlockSpec((1, LANES), fixed2),
                  pl.BlockSpec((1, LANES), fixed2),
                  pl.BlockSpec((1, DN_DV), fixed2)],
        out_specs=[pl.BlockSpec((bb, L, D_A), lambda i, n: (i, n, 0)), sspec],
        out_shape=[jax.ShapeDtypeStruct((b, t, D_A), BF16),
                   jax.ShapeDtypeStruct((b, DN_HEADS, DN_DK, DN_DV), F32)],
        scratch_shapes=[pltpu.VMEM((bb * DN_HEADS, DN_DK, DN_DV), F32)],
        compiler_params=pltpu.CompilerParams(dimension_semantics=("parallel", "arbitrary"),
                                             vmem_limit_bytes=VMEM_LIMIT),
        name="dn",
    )(proj3, proj3, proj3, proj3, proj3, s0, alog_row, dtb_row, nrm_row)


def _s5prep_kernel(a_ref, th_ref, lr_ref, li_ref, bre_ref, bim_ref, a16_ref, th16_ref,
                   pnr_ref, pni_ref, ppr_ref, ppi_ref, l1r_ref, l1i_ref, bbr_ref, bbi_ref, *, L):
    tcol = lax.broadcasted_iota(jnp.int32, (L, 1), 0).astype(F32)
    a = a_ref[...]
    th = th_ref[...]
    ppr_ref[...] = jnp.exp(a * tcol) * jnp.cos(th * tcol)
    ppi_ref[...] = jnp.exp(a * tcol) * jnp.sin(th * tcol)
    pnr_ref[...] = jnp.exp(-a * tcol) * jnp.cos(th * tcol)
    pni_ref[...] = -jnp.exp(-a * tcol) * jnp.sin(th * tcol)
    l1r_ref[...] = jnp.exp(a) * jnp.cos(th)
    l1i_ref[...] = jnp.exp(a) * jnp.sin(th)
    a16 = a16_ref[...]
    th16 = th16_ref[...]
    lr = lr_ref[...]
    li = li_ref[...]
    nr = jnp.exp(a16) * jnp.cos(th16) - 1.0
    ni = jnp.exp(a16) * jnp.sin(th16)
    den = lr * lr + li * li
    cr = (nr * lr + ni * li) / den
    ci = (ni * lr - nr * li) / den
    bre = bre_ref[...]
    bim = bim_ref[...]
    bbr_ref[...] = cr * bre - ci * bim
    bbi_ref[...] = cr * bim + ci * bre


def _s5prep(lam_re, lam_im, log_dt, b_re, b_im, L):
    dt = jnp.exp(log_dt)[:, None]
    a_row = (lam_re * dt).reshape(1, N_SSM)
    th_row = (lam_im * dt).reshape(1, N_SSM)
    rep = lambda x: jnp.broadcast_to(x.reshape(N_SSM, 1), (N_SSM, SSM_GROUP))
    outs = pl.pallas_call(
        functools.partial(_s5prep_kernel, L=L),
        out_shape=[jax.ShapeDtypeStruct((L, N_SSM), F32)] * 4
        + [jax.ShapeDtypeStruct((1, N_SSM), F32)] * 2
        + [jax.ShapeDtypeStruct((N_SSM, SSM_GROUP), F32)] * 2,
        name="s5prep",
    )(a_row, th_row, rep(lam_re), rep(lam_im), b_re.reshape(N_SSM, SSM_GROUP), b_im.reshape(N_SSM, SSM_GROUP),
      rep(lam_re * dt), rep(lam_im * dt))
    return outs


def _s5_kernel(u_ref, z_ref, h0r_ref, h0i_ref, pnr_ref, pni_ref, ppr_ref, ppi_ref, l1r_ref, l1i_ref,
               bre_ref, bim_ref, cre_ref, cim_ref, d_ref, gw_ref, gb_ref,
               ob_ref, hr_out, hi_out, hr_scr, hi_scr, hbr_scr, hbi_scr, *, L, TT):
    n = pl.program_id(1)
    nc = pl.num_programs(1)

    @pl.when(n == 0)
    def _():
        hr_scr[...] = h0r_ref[0]
        hi_scr[...] = h0i_ref[0]

    _, _, incl, _ = _tri_masks(L)
    tri = incl.astype(BF16)
    u = u_ref[0]
    ub = u.astype(BF16)
    halves = range(S5_HALVES)
    kh = D_B // S5_HALVES
    nh = N_SSM // S5_HALVES
    ncb = N_SSM // S5_COLS
    bus = []
    for cb in range(ncb):
        a = (cb * S5_COLS) // nh
        lc = cb * S5_COLS - a * nh
        ua = ub[:, a * kh:(a + 1) * kh]
        bus.append((jnp.dot(ua, bre_ref[a, :, lc:lc + S5_COLS], preferred_element_type=F32),
                    jnp.dot(ua, bim_ref[a, :, lc:lc + S5_COLS], preferred_element_type=F32)))
    for cb in range(ncb):
        cs = slice(cb * S5_COLS, (cb + 1) * S5_COLS)
        bur, bui = bus[cb]
        l1r = l1r_ref[:, cs]
        l1i = l1i_ref[:, cs]
        hr0 = hr_scr[:, cs]
        hi0 = hi_scr[:, cs]
        for j in range(TT // L):
            rows = slice(j * L, (j + 1) * L)
            pnr = pnr_ref[:, cs]
            pni = pni_ref[:, cs]
            vr = pnr * bur[rows] - pni * bui[rows]
            vi = pnr * bui[rows] + pni * bur[rows]
            cr = _dot(tri, vr) + (l1r * hr0 - l1i * hi0)
            ci = _dot(tri, vi) + (l1r * hi0 + l1i * hr0)
            ppr = ppr_ref[:, cs]
            ppi = ppi_ref[:, cs]
            hr = ppr * cr - ppi * ci
            hi = ppr * ci + ppi * cr
            hr0 = hr[L - 1:L, :]
            hi0 = hi[L - 1:L, :]
            hbr_scr[rows, cs] = hr.astype(BF16)
            hbi_scr[rows, cs] = hi.astype(BF16)
        hr_scr[:, cs] = hr0
        hi_scr[:, cs] = hi0
    y = jnp.concatenate([jnp.dot(hbr_scr[:, a * nh:(a + 1) * nh], cre_ref[a], preferred_element_type=F32)
                         - jnp.dot(hbi_scr[:, a * nh:(a + 1) * nh], cim_ref[a], preferred_element_type=F32)
                         for a in halves], axis=1) + d_ref[...] * u
    c0 = math.sqrt(2.0 / math.pi)
    y = 0.5 * y * (1.0 + jnp.tanh(c0 * (y + 0.044715 * (y * y * y))))
    y = y * _sigmoid(_dot(y, gw_ref[...]) + gb_ref[...])
    ob_ref[0] = (y * z_ref[0]).astype(BF16)

    @pl.when(n == nc - 1)
    def _():
        hr_out[0] = hr_scr[...]
        hi_out[0] = hi_scr[...]


def _s5(proj3, h0r, h0i, tabs, bbd_re, bbd_im, cbd_re, cbd_im, d_row, gw, gb_row, L):
    b, t, _ = proj3.shape
    tt = min(S5_ROWS, t)
    pnr, pni, ppr, ppi, l1r, l1i = tabs
    full = lambda a: pl.BlockSpec(a.shape, lambda i, n: (0,) * a.ndim)
    state = pl.BlockSpec((1, 1, N_SSM), lambda i, n: (i, 0, 0))
    u_col = COL_BU // D_B
    return pl.pallas_call(
        functools.partial(_s5_kernel, L=L, TT=tt),
        grid=(b, t // tt),
        in_specs=[pl.BlockSpec((1, tt, D_B), lambda i, n: (i, n, u_col)),
                  pl.BlockSpec((1, tt, D_B), lambda i, n: (i, n, u_col + 1)),
                  state, state,
                  full(pnr), full(pni), full(ppr), full(ppi), full(l1r), full(l1i),
                  full(bbd_re), full(bbd_im), full(cbd_re), full(cbd_im), full(d_row), full(gw), full(gb_row)],
        out_specs=[pl.BlockSpec((1, tt, D_B), lambda i, n: (i, n, 0)), state, state],
        out_shape=[jax.ShapeDtypeStruct((b, t, D_B), BF16),
                   jax.ShapeDtypeStruct((b, 1, N_SSM), F32),
                   jax.ShapeDtypeStruct((b, 1, N_SSM), F32)],
        scratch_shapes=[pltpu.VMEM((1, N_SSM), F32), pltpu.VMEM((1, N_SSM), F32),
                        pltpu.VMEM((tt, N_SSM), BF16), pltpu.VMEM((tt, N_SSM), BF16)],
        compiler_params=pltpu.CompilerParams(dimension_semantics=("parallel", "arbitrary"),
                                             vmem_limit_bytes=VMEM_LIMIT),
        name="s5",
    )(proj3, proj3, h0r, h0i, pnr, pni, ppr, ppi, l1r, l1i, bbd_re, bbd_im, cbd_re, cbd_im, d_row, gw, gb_row)


def _ml_kernel(q_ref, k_ref, v_ref, o_ref, z_ref, sm_ref, c0_ref, n0_ref, m0_ref, ib_ref, fb_ref, nrm_ref,
               oc_ref, cout_ref, nout_ref, mout_ref, c_scr, n_scr, m_scr, *, L, BB):
    n = pl.program_id(1)
    nc = pl.num_programs(1)

    @pl.when(n == 0)
    def _():
        for r in range(BB):
            c_scr[r * ML_HEADS:(r + 1) * ML_HEADS] = c0_ref[r]
        n_scr[...] = n0_ref[...]
        m_scr[...] = m0_ref[...]

    _, _, incl, _ = _tri_masks(L)
    lane = lax.broadcasted_iota(jnp.int32, (1, LANES), 1)
    pairs = [(r, h) for r in range(BB) for h in range(ML_HEADS)]
    sl = lambda h: slice(h * ML_DH, (h + 1) * ML_DH)

    b_l, imb_l, imbrow_l, mprev_l, nvec_l = [], [], [], [], []
    for r in range(BB):
        sm = sm_ref[r]
        ipre_all = sm + ib_ref[...]
        xf = sm + fb_ref[...]
        logf_all = jnp.minimum(xf, 0.0) - jnp.log(1.0 + jnp.exp(-jnp.abs(xf)))
        b_all = _cumsum_rows(logf_all)
        imb_all = pltpu.roll(ipre_all, SM_F - SM_I, axis=1) - b_all
        imb_t = _transpose_exact(imb_all)
        m_all = m_scr[r]
        n_all = n_scr[r]
        for h in range(ML_HEADS):
            b_l.append(b_all[:, SM_F + h:SM_F + h + 1])
            imb_l.append(imb_all[:, SM_F + h:SM_F + h + 1])
            imbrow_l.append(imb_t[SM_F + h:SM_F + h + 1, :])
            mprev_l.append(m_all[:, h:h + 1])
            nvec_l.append(n_all[h:h + 1, :])

    q = jnp.stack([q_ref[r, :, sl(h)] for r, h in pairs])
    k = jnp.stack([k_ref[r, :, sl(h)] for r, h in pairs])
    v = jnp.stack([v_ref[r, :, sl(h)] for r, h in pairs])
    b = jnp.stack(b_l)
    imb = jnp.stack(imb_l)
    imb_row = jnp.stack(imbrow_l)
    m_prev = jnp.stack(mprev_l)
    nvec = jnp.stack(nvec_l)
    logw = jnp.where(incl, b + imb_row, -jnp.inf)
    lwm = jnp.max(logw, axis=-1, keepdims=True)
    m_t = jnp.maximum(b + m_prev, lwm)
    w_state = jnp.exp(b + m_prev - m_t)
    w_intra = jnp.exp(logw - m_t) * _bdot_nt(q, k)
    c = c_scr[...]
    num = w_state * _bdot(q, c) + _bdot(w_intra, v)
    den = w_state * jnp.sum(q * nvec, axis=-1, keepdims=True) + jnp.sum(w_intra, axis=-1, keepdims=True)
    hh = num / jnp.maximum(jnp.abs(den), jnp.exp(-m_t))
    m_new = m_t[:, L - 1:L, :]
    b_last = b[:, L - 1:L, :]
    w_keep = jnp.exp(b_last + m_prev - m_new)
    w_end = jnp.exp(imb + (b_last - m_new))
    kw = k * w_end
    c_scr[...] = w_keep * c + _bdot_tn(kw, v)
    n_new = w_keep * nvec + jnp.sum(kw, axis=1, keepdims=True)
    for r in range(BB):
        m_next = m_scr[r]
        for h in range(ML_HEADS):
            idx = r * ML_HEADS + h
            n_scr[r, h:h + 1, :] = n_new[idx]
            m_next = jnp.where(lane == h, m_new[idx], m_next)
            gated = o_ref[r, :, sl(h)] * hh[idx]
            oc_ref[r, :, sl(h)] = (_rms(gated, nrm_ref[...]) * z_ref[r, :, sl(h)]).astype(BF16)
        m_scr[r] = m_next

    @pl.when(n == nc - 1)
    def _():
        for r in range(BB):
            cout_ref[r] = c_scr[r * ML_HEADS:(r + 1) * ML_HEADS]
        nout_ref[...] = n_scr[...]
        mout_ref[...] = m_scr[...]


def _ml(proj3, c0, n0, m0p, ib_row, fb_row, nrm_row, L):
    b, t, _ = proj3.shape
    bb = ML_ROWS
    c_col = COL_CQ // D_C
    col = lambda c: (lambda i, n: (i, n, c))
    fixed2 = lambda i, n: (0, 0)
    cspec = pl.BlockSpec((bb, ML_HEADS, ML_DH, ML_DH), lambda i, n: (i, 0, 0, 0))
    nspec = pl.BlockSpec((bb, ML_HEADS, ML_DH), lambda i, n: (i, 0, 0))
    mspec = pl.BlockSpec((bb, 1, LANES), lambda i, n: (i, 0, 0))
    return pl.pallas_call(
        functools.partial(_ml_kernel, L=L, BB=bb),
        grid=(b // bb, t // L),
        in_specs=[pl.BlockSpec((bb, L, D_C), col(c_col + j)) for j in range(5)]
        + [pl.BlockSpec((bb, L, LANES), col(COL_SMALL // LANES)),
           cspec, nspec, mspec,
           pl.BlockSpec((1, LANES), fixed2), pl.BlockSpec((1, LANES), fixed2), pl.BlockSpec((1, ML_DH), fixed2)],
        out_specs=[pl.BlockSpec((bb, L, D_C), lambda i, n: (i, n, 0)), cspec, nspec, mspec],
        out_shape=[jax.ShapeDtypeStruct((b, t, D_C), BF16),
                   jax.ShapeDtypeStruct((b, ML_HEADS, ML_DH, ML_DH), F32),
                   jax.ShapeDtypeStruct((b, ML_HEADS, ML_DH), F32),
                   jax.ShapeDtypeStruct((b, 1, LANES), F32)],
        scratch_shapes=[pltpu.VMEM((bb * ML_HEADS, ML_DH, ML_DH), F32),
                        pltpu.VMEM((bb, ML_HEADS, ML_DH), F32),
                        pltpu.VMEM((bb, 1, LANES), F32)],
        compiler_params=pltpu.CompilerParams(dimension_semantics=("parallel", "arbitrary"),
                                             vmem_limit_bytes=VMEM_LIMIT),
        name="ml",
    )(proj3, proj3, proj3, proj3, proj3, proj3, c0, n0, m0p, ib_row, fb_row, nrm_row)


def _out_kernel(oa_ref, ob_ref, oc_ref, w_ref, x_ref, g_ref, y_ref):
    mix = jnp.concatenate([oa_ref[...], ob_ref[...], oc_ref[...]], axis=1)
    acc = jnp.dot(mix, w_ref[...], preferred_element_type=F32)
    y_ref[...] = x_ref[...] + _rms(acc, g_ref[...])


def _outproj(oa, ob, oc, w_bf, x2, g_row, tm):
    n = x2.shape[0]
    rows = lambda w: pl.BlockSpec((tm, w), lambda i: (i, 0))
    full = lambda a: pl.BlockSpec(a.shape, lambda i: (0, 0))
    return pl.pallas_call(
        _out_kernel,
        grid=(n // tm,),
        in_specs=[rows(D_A), rows(D_B), rows(D_C), full(w_bf), rows(D_MODEL), full(g_row)],
        out_specs=rows(D_MODEL),
        out_shape=jax.ShapeDtypeStruct((n, D_MODEL), F32),
        compiler_params=pltpu.CompilerParams(dimension_semantics=("parallel",),
                                             vmem_limit_bytes=VMEM_LIMIT),
        name="outproj",
    )(oa, ob, oc, w_bf, x2, g_row)


def _pad_lanes(v, offset):
    return jnp.zeros((1, LANES), F32).at[0, offset:offset + v.shape[0]].set(v.astype(F32))


def _block_diag_halves(blocks):
    g, r, c = blocks.shape
    gh = g // S5_HALVES
    eye = jnp.eye(gh, dtype=blocks.dtype)
    bd = jnp.einsum("agrc,gh->agrhc", blocks.reshape(S5_HALVES, gh, r, c), eye)
    return bd.reshape(S5_HALVES, gh * r, gh * c).astype(BF16)


def _prep_layer_weights(w):
    (norm_pre, norm_post, w_in, dn_conv_w, dn_A_log, dn_dt_bias, dn_norm, ssm_lam_re, ssm_lam_im, ssm_log_dt,
     ssm_B_re, ssm_B_im, ssm_C_re, ssm_C_im, ssm_D, ssm_glu_w, ssm_glu_b, ml_i_bias, ml_f_bias, ml_norm, w_out) = w
    o = 0
    seg = {}
    for name, size in (("qkv", 3 * D_A), ("az", D_A), ("beta", DN_HEADS), ("alpha", DN_HEADS), ("bu", D_B),
                       ("bz", D_B), ("cq", D_C), ("ck", D_C), ("cv", D_C), ("co", D_C), ("cz", D_C),
                       ("ci", ML_HEADS), ("cf", ML_HEADS)):
        seg[name] = w_in[:, o:o + size]
        o += size
    n_small = 2 * DN_HEADS + 2 * ML_HEADS
    w_re = jnp.concatenate(
        [seg[k] for k in ("qkv", "az", "bu", "bz", "cq", "ck", "cv", "co", "cz", "beta", "alpha", "ci", "cf")]
        + [jnp.zeros((D_MODEL, LANES - n_small), w_in.dtype)], axis=1).astype(BF16)
    cbd = lambda c: _block_diag_halves(jnp.swapaxes(c.astype(F32), 1, 2))
    return dict(
        norm_pre=norm_pre.reshape(1, D_MODEL).astype(F32), norm_post=norm_post.reshape(1, D_MODEL).astype(F32),
        w_in=w_re, conv_w=dn_conv_w.astype(F32),
        alog_row=_pad_lanes(dn_A_log, SM_ALPHA), dtb_row=_pad_lanes(dn_dt_bias, SM_ALPHA),
        dn_norm=dn_norm.reshape(1, DN_DV).astype(F32),
        lam_re=ssm_lam_re.astype(F32), lam_im=ssm_lam_im.astype(F32), log_dt=ssm_log_dt.astype(F32),
        b_re=ssm_B_re.astype(F32), b_im=ssm_B_im.astype(F32), cbd_re=cbd(ssm_C_re), cbd_im=cbd(ssm_C_im),
        d_row=ssm_D.reshape(1, D_B).astype(F32), glu_w=ssm_glu_w.astype(BF16),
        glu_b=ssm_glu_b.reshape(1, D_B).astype(F32),
        ib_row=_pad_lanes(ml_i_bias, SM_I), fb_row=_pad_lanes(ml_f_bias, SM_F),
        ml_norm=ml_norm.reshape(1, ML_DH).astype(F32),
        w_out=w_out.astype(BF16))


def _s5_tables(p, L):
    pnr, pni, ppr, ppi, l1r, l1i, bbr, bbi = _s5prep(p["lam_re"], p["lam_im"], p["log_dt"], p["b_re"], p["b_im"], L)
    bbd = lambda bb: _block_diag_halves(jnp.swapaxes(bb.reshape(SSM_GROUPS, SSM_STATE, SSM_GROUP), 1, 2))
    return (pnr, pni, ppr, ppi, l1r, l1i), bbd(bbr), bbd(bbi)


def _layer(x, state, p, s5tab, L):
    conv0, s0, h0r, h0i, c0, n0, m0 = state
    b, t, _ = x.shape
    x2 = x.reshape(b * t, D_MODEL)
    conv0p = jnp.pad(conv0.astype(F32), ((0, 0), (SUBLANES - (CONV_W - 1), 0), (0, 0)))
    proj2, conv_tail = _proj(x2, p["norm_pre"], p["w_in"], p["conv_w"], conv0p, min(PROJ_ROWS, t), t)
    proj3 = proj2.reshape(b, t, DP)

    oa, s_new = _dn(proj3, s0.astype(F32), p["alog_row"], p["dtb_row"], p["dn_norm"], L)
    tabs, bbd_re, bbd_im = s5tab
    ob, hr_new, hi_new = _s5(proj3, h0r.reshape(b, 1, N_SSM).astype(F32), h0i.reshape(b, 1, N_SSM).astype(F32),
                             tabs, bbd_re, bbd_im, p["cbd_re"], p["cbd_im"], p["d_row"], p["glu_w"], p["glu_b"], L)
    m0p = jnp.pad(m0.astype(F32), ((0, 0), (0, LANES - ML_HEADS))).reshape(b, 1, LANES)
    oc, c_new, n_new, m_new = _ml(proj3, c0.astype(F32), n0.astype(F32), m0p, p["ib_row"], p["fb_row"],
                                  p["ml_norm"], L)
    y2 = _outproj(oa.reshape(b * t, D_A), ob.reshape(b * t, D_B), oc.reshape(b * t, D_C),
                  p["w_out"], x2, p["norm_post"], min(OUT_ROWS, b * t))
    new_state = (conv_tail[:, SUBLANES - (CONV_W - 1):, :], s_new,
                 hr_new.reshape(b, SSM_GROUPS, SSM_STATE), hi_new.reshape(b, SSM_GROUPS, SSM_STATE),
                 c_new, n_new, m_new[:, 0, :ML_HEADS])
    return y2.reshape(b, t, D_MODEL), new_state


def kernel(x_prompt, x_sample, state_dn_conv, state_dn_S, state_ssm_re, state_ssm_im, state_ml_C, state_ml_n, state_ml_m, norm_pre, norm_post, w_in, dn_conv_w, dn_A_log, dn_dt_bias, dn_norm, ssm_lam_re, ssm_lam_im, ssm_log_dt, ssm_B_re, ssm_B_im, ssm_C_re, ssm_C_im, ssm_D, ssm_glu_w, ssm_glu_b, ml_i_bias, ml_f_bias, ml_norm, w_out):
    bp, tp, _ = x_prompt.shape
    bs, ts, _ = x_sample.shape
    lp = min(CHUNK, tp)
    ls = min(CHUNK, ts)
    zero_state = (jnp.zeros((bp, CONV_W - 1, 3 * D_A), F32),
                  jnp.zeros((bp, DN_HEADS, DN_DK, DN_DV), F32),
                  jnp.zeros((bp, SSM_GROUPS, SSM_STATE), F32),
                  jnp.zeros((bp, SSM_GROUPS, SSM_STATE), F32),
                  jnp.zeros((bp, ML_HEADS, ML_DH, ML_DH), F32),
                  jnp.zeros((bp, ML_HEADS, ML_DH), F32),
                  jnp.zeros((bp, ML_HEADS), F32))
    weights = (norm_pre, norm_post, w_in, dn_conv_w, dn_A_log, dn_dt_bias, dn_norm, ssm_lam_re, ssm_lam_im,
               ssm_log_dt, ssm_B_re, ssm_B_im, ssm_C_re, ssm_C_im, ssm_D, ssm_glu_w, ssm_glu_b, ml_i_bias,
               ml_f_bias, ml_norm, w_out)
    xp, xs = x_prompt, x_sample
    new_p, new_s = [], []
    for l in range(DEPTH):
        p = _prep_layer_weights(tuple(w[l] for w in weights))
        (pnr, pni, ppr, ppi, l1r, l1i), bbd_re, bbd_im = _s5_tables(p, max(lp, ls))
        tabs = lambda L: ((pnr[:L], pni[:L], ppr[:L], ppi[:L], l1r, l1i), bbd_re, bbd_im)
        xp, sp = _layer(xp, zero_state, p, tabs(lp), lp)
        carried = (state_dn_conv[l], state_dn_S[l], state_ssm_re[l], state_ssm_im[l],
                   state_ml_C[l], state_ml_n[l], state_ml_m[l])
        xs, ss = _layer(xs, carried, p, tabs(ls), ls)
        new_p.append(sp)
        new_s.append(ss)
    stack = lambda states, i, dt: jnp.stack([s[i] for s in states], axis=0).astype(dt)
    pst = [stack(new_p, i, x_prompt.dtype) for i in range(7)]
    sst = [stack(new_s, i, x_sample.dtype) for i in range(7)]
    return (xp, xs, pst[0], pst[1], pst[2], pst[3], pst[4], pst[5], pst[6],
            sst[0], sst[1], sst[2], sst[3], sst[4], sst[5], sst[6])
```

```python
import functools
import math

import jax
import jax.numpy as jnp
from jax import lax
from jax.experimental import pallas as pl
from jax.experimental.pallas import tpu as pltpu

F32 = jnp.float32
BF16 = jnp.bfloat16

D_MODEL = 1024
DEPTH = 2
CHUNK = 64
D_MIX = 2 * D_MODEL
D_A = D_MIX // 2
DN_DK = 128
DN_DV = 128
DN_HEADS = D_A // DN_DV
CONV_W = 4
D_B = D_MIX // 4
SSM_GROUP = 16
SSM_GROUPS = D_B // SSM_GROUP
SSM_STATE = 64
D_C = D_MIX // 4
ML_DH = 128
ML_HEADS = D_C // ML_DH
EPS = 1e-6
N_SSM = SSM_GROUPS * SSM_STATE
S5_HALVES = 2
S5_ROWS = 512
S5_COLS = 256
PROJ_ROWS = 256
OUT_ROWS = 1024
DN_ROWS = 4
ML_ROWS = 8

LANES = 128
SUBLANES = 8

COL_Q, COL_K, COL_V = 0, D_A, 2 * D_A
COL_AZ = 3 * D_A
COL_BU = COL_AZ + D_A
COL_BZ = COL_BU + D_B
COL_CQ = COL_BZ + D_B
COL_SMALL = COL_CQ + 5 * D_C
DP = COL_SMALL + LANES
SM_BETA = 0
SM_ALPHA = DN_HEADS
SM_I = 2 * DN_HEADS
SM_F = 2 * DN_HEADS + ML_HEADS
PROJ_TILE = 512
_PROJ_TILES = tuple(
    [(COL_Q + c, PROJ_TILE, "q") for c in range(0, D_A, PROJ_TILE)]
    + [(COL_K + c, PROJ_TILE, "k") for c in range(0, D_A, PROJ_TILE)]
    + [(COL_V + c, PROJ_TILE, "v") for c in range(0, D_A, PROJ_TILE)]
    + [(COL_AZ + c, PROJ_TILE, "silu") for c in range(0, D_A, PROJ_TILE)]
    + [(COL_BU, D_B, "none"), (COL_BZ, D_B, "silu"),
       (COL_CQ, D_C, "none"), (COL_CQ + D_C, D_C, "kscale"), (COL_CQ + 2 * D_C, D_C, "none"),
       (COL_CQ + 3 * D_C, D_C, "sigmoid"), (COL_CQ + 4 * D_C, D_C, "silu"),
       (COL_SMALL, LANES, "none")])

VMEM_LIMIT = 48 * 1024 * 1024


def _dot(a, b):
    return jnp.dot(a.astype(BF16), b.astype(BF16), preferred_element_type=F32)


def _bdot(a, b):
    return lax.dot_general(a.astype(BF16), b.astype(BF16), (((2,), (1,)), ((0,), (0,))),
                           preferred_element_type=F32)


def _bdot_nt(a, b):
    return lax.dot_general(a.astype(BF16), b.astype(BF16), (((2,), (2,)), ((0,), (0,))),
                           preferred_element_type=F32)


def _bdot_tn(a, b):
    return lax.dot_general(a.astype(BF16), b.astype(BF16), (((1,), (1,)), ((0,), (0,))),
                           preferred_element_type=F32)


def _cumsum_rows(x):
    n = x.shape[0]
    row = lax.broadcasted_iota(jnp.int32, x.shape, 0)
    d = 1
    while d < n:
        x = x + jnp.where(row >= d, pltpu.roll(x, d, axis=0), 0.0)
        d *= 2
    return x


def _transpose_exact(x):
    r = x.shape[0]
    if r < LANES:
        x = jnp.concatenate([x, jnp.zeros((LANES - r, LANES), x.dtype)], axis=0)
    return x.T[:, :r]


def _sigmoid(x):
    return 1.0 / (1.0 + jnp.exp(-x))


def _silu(x):
    return x * _sigmoid(x)


def _softplus(x):
    return jnp.maximum(x, 0.0) + jnp.log(1.0 + jnp.exp(-jnp.abs(x)))


def _tri_masks(L):
    row = lax.broadcasted_iota(jnp.int32, (L, L), 0)
    col = lax.broadcasted_iota(jnp.int32, (L, L), 1)
    return row, col, row >= col, row > col


def _rms(x, g):
    return x * lax.rsqrt(jnp.mean(x * x, axis=-1, keepdims=True) + EPS) * g


def _proj_kernel(x_ref, g_ref, w_ref, cw_ref, conv0_ref, o_ref, ctail_ref, carry_scr, *, tm, tiles_per_seq):
    @pl.when(pl.program_id(0) % tiles_per_seq == 0)
    def _():
        carry_scr[...] = conv0_ref[0]

    h = _rms(x_ref[...], g_ref[...]).astype(BF16)
    for c0, width, kind in _PROJ_TILES:
        cols = slice(c0, c0 + width)
        acc = jnp.dot(h, w_ref[:, cols], preferred_element_type=F32)
        if kind in ("q", "k", "v"):
            ext = jnp.concatenate([carry_scr[:, cols], acc], axis=0)
            tail = acc[tm - SUBLANES:tm, :]
            carry_scr[:, cols] = tail
            ctail_ref[0, :, cols] = tail
            y = acc * cw_ref[CONV_W - 1:CONV_W, cols]
            for d in range(1, CONV_W):
                shifted = pltpu.roll(ext, d, axis=0)[SUBLANES:SUBLANES + tm, :]
                y = y + shifted * cw_ref[CONV_W - 1 - d:CONV_W - d, cols]
            y = _silu(y)
            if kind != "v":
                parts = []
                for s in range(width // DN_DK):
                    yh = y[:, s * DN_DK:(s + 1) * DN_DK]
                    yh = yh * lax.rsqrt(jnp.sum(yh * yh, axis=-1, keepdims=True) + EPS)
                    parts.append(yh * (DN_DK ** -0.5) if kind == "q" else yh)
                y = jnp.concatenate(parts, axis=1)
            o_ref[:, cols] = y
        elif kind == "silu":
            o_ref[:, cols] = _silu(acc)
        elif kind == "sigmoid":
            o_ref[:, cols] = _sigmoid(acc)
        elif kind == "kscale":
            o_ref[:, cols] = acc * (ML_DH ** -0.5)
        else:
            o_ref[:, cols] = acc


def _proj(x2, g_row, w_bf, cw, conv0p, tm, seq_len):
    n = x2.shape[0]
    nseq = conv0p.shape[0]
    tps = seq_len // tm
    return pl.pallas_call(
        functools.partial(_proj_kernel, tm=tm, tiles_per_seq=tps),
        grid=(n // tm,),
        in_specs=[pl.BlockSpec((tm, D_MODEL), lambda i: (i, 0)),
                  pl.BlockSpec((1, D_MODEL), lambda i: (0, 0)),
                  pl.BlockSpec((D_MODEL, DP), lambda i: (0, 0), pipeline_mode=pl.Buffered(1)),
                  pl.BlockSpec((CONV_W, 3 * D_A), lambda i: (0, 0)),
                  pl.BlockSpec((1, SUBLANES, 3 * D_A), lambda i: (i // tps, 0, 0))],
        out_specs=[pl.BlockSpec((tm, DP), lambda i: (i, 0)),
                   pl.BlockSpec((1, SUBLANES, 3 * D_A), lambda i: (i // tps, 0, 0))],
        out_shape=[jax.ShapeDtypeStruct((n, DP), F32),
                   jax.ShapeDtypeStruct((nseq, SUBLANES, 3 * D_A), F32)],
        scratch_shapes=[pltpu.VMEM((SUBLANES, 3 * D_A), F32)],
        compiler_params=pltpu.CompilerParams(dimension_semantics=("arbitrary",),
                                             vmem_limit_bytes=VMEM_LIMIT),
        name="proj",
    )(x2, g_row, w_bf, cw, conv0p)


def _unit_lower_inverse(a_strict, row, col, L):
    t = None
    s = 1
    while s < L:
        sh = s.bit_length() - 1
        m = ((row >> (sh + 1)) == (col >> (sh + 1))) & (((row >> sh) & 1) == 1) & (((col >> sh) & 1) == 0)
        a_off = jnp.where(m, a_strict, 0.0)
        if t is None:
            t = jnp.where(row == col, 1.0, 0.0) - a_off
        else:
            t = t - _bdot(t, _bdot(a_off, t))
        s *= 2
    return t


def _dn_kernel(q_ref, k_ref, v_ref, z_ref, sm_ref, s0_ref, alog_ref, dtb_ref, nrm_ref,
               oa_ref, sout_ref, s_scr, *, L, BB):
    n = pl.program_id(1)
    nc = pl.num_programs(1)

    @pl.when(n == 0)
    def _():
        for r in range(BB):
            s_scr[r * DN_HEADS:(r + 1) * DN_HEADS] = s0_ref[r]

    row, col, incl, strict = _tri_masks(L)

    beta_l, g_l, grow_l = [], [], []
    for r in range(BB):
        sm = sm_ref[r]
        beta_all = _sigmoid(sm)
        la_all = -jnp.exp(alog_ref[...]) * _softplus(sm + dtb_ref[...])
        g_all = _cumsum_rows(la_all)
        g_t = _transpose_exact(g_all)
        for h in range(DN_HEADS):
            beta_l.append(beta_all[:, SM_BETA + h:SM_BETA + h + 1])
            g_l.append(g_all[:, SM_ALPHA + h:SM_ALPHA + h + 1])
            grow_l.append(g_t[SM_ALPHA + h:SM_ALPHA + h + 1, :])

    pairs = [(r, h) for r in range(BB) for h in range(DN_HEADS)]
    hs = lambda h: slice(h * DN_DK, (h + 1) * DN_DK)
    q = jnp.stack([q_ref[r, :, hs(h)] for r, h in pairs])
    k = jnp.stack([k_ref[r, :, hs(h)] for r, h in pairs])
    v = jnp.stack([v_ref[r, :, hs(h)] for r, h in pairs])
    beta = jnp.stack(beta_l)
    g = jnp.stack(g_l)
    g_row = jnp.stack(grow_l)
    g_last = g[:, L - 1:L, :]
    eg = jnp.exp(g)
    decay = jnp.where(incl, jnp.exp(jnp.where(incl, g - g_row, 0.0)), 0.0)
    kq_k = _bdot_nt(jnp.concatenate([k, q], axis=1), k)
    a_mat = jnp.where(strict, beta * kq_k[:, :L, :] * decay, 0.0)
    attn = kq_k[:, L:, :] * decay
    t_inv = _unit_lower_inverse(a_mat, row, col, L)
    rhs = jnp.concatenate([k * (beta * eg), v * beta], axis=-1)
    wu = _bdot(t_inv, rhs)
    w = wu[:, :, :DN_DK]
    u = wu[:, :, DN_DK:]
    s = s_scr[...]
    wq_s = _bdot(jnp.concatenate([w, q * eg], axis=1), s)
    delta = u - wq_s[:, :L, :]
    o = wq_s[:, L:, :] + _bdot(attn, delta)
    s_scr[...] = jnp.exp(g_last) * s + _bdot_tn(k * jnp.exp(g_last - g), delta)
    o = _rms(o, nrm_ref[...])
    for idx, (r, h) in enumerate(pairs):
        oa_ref[r, :, hs(h)] = (o[idx] * z_ref[r, :, hs(h)]).astype(BF16)

    @pl.when(n == nc - 1)
    def _():
        for r in range(BB):
            sout_ref[r] = s_scr[r * DN_HEADS:(r + 1) * DN_HEADS]


def _dn(proj3, s0, alog_row, dtb_row, nrm_row, L):
    b, t, _ = proj3.shape
    bb = DN_ROWS
    col = lambda c: (lambda i, n: (i, n, c))
    fixed2 = lambda i, n: (0, 0)
    sspec = pl.BlockSpec((bb, DN_HEADS, DN_DK, DN_DV), lambda i, n: (i, 0, 0, 0))
    return pl.pallas_call(
        functools.partial(_dn_kernel, L=L, BB=bb),
        grid=(b // bb, t // L),
        in_specs=[pl.BlockSpec((bb, L, D_A), col(COL_Q // D_A)),
                  pl.BlockSpec((bb, L, D_A), col(COL_K // D_A)),
                  pl.BlockSpec((bb, L, D_A), col(COL_V // D_A)),
                  pl.BlockSpec((bb, L, D_A), col(COL_AZ // D_A)),
                  pl.BlockSpec((bb, L, LANES), col(COL_SMALL // LANES)),
                  sspec,
                  pl.BlockSpec((1, LANES), fixed2),
                  pl.BlockSpec((1, LANES), fixed2),
                  pl.BlockSpec((1, DN_DV), fixed2)],
        out_specs=[pl.BlockSpec((bb, L, D_A), lambda i, n: (i, n, 0)), sspec],
        out_shape=[jax.ShapeDtypeStruct((b, t, D_A), BF16),
                   jax.ShapeDtypeStruct((b, DN_HEADS, DN_DK, DN_DV), F32)],
        scratch_shapes=[pltpu.VMEM((bb * DN_HEADS, DN_DK, DN_DV), F32)],
        compiler_params=pltpu.CompilerParams(dimension_semantics=("parallel", "arbitrary"),
                                             vmem_limit_bytes=VMEM_LIMIT),
        name="dn",
    )(proj3, proj3, proj3, proj3, proj3, s0, alog_row, dtb_row, nrm_row)


def _s5prep_kernel(a_ref, th_ref, lr_ref, li_ref, bre_ref, bim_ref, a16_ref, th16_ref,
                   pnr_ref, pni_ref, ppr_ref, ppi_ref, l1r_ref, l1i_ref, bbr_ref, bbi_ref, *, L):
    tcol = lax.broadcasted_iota(jnp.int32, (L, 1), 0).astype(F32)
    a = a_ref[...]
    th = th_ref[...]
    ppr_ref[...] = jnp.exp(a * tcol) * jnp.cos(th * tcol)
    ppi_ref[...] = jnp.exp(a * tcol) * jnp.sin(th * tcol)
    pnr_ref[...] = jnp.exp(-a * tcol) * jnp.cos(th * tcol)
    pni_ref[...] = -jnp.exp(-a * tcol) * jnp.sin(th * tcol)
    l1r_ref[...] = jnp.exp(a) * jnp.cos(th)
    l1i_ref[...] = jnp.exp(a) * jnp.sin(th)
    a16 = a16_ref[...]
    th16 = th16_ref[...]
    lr = lr_ref[...]
    li = li_ref[...]
    nr = jnp.exp(a16) * jnp.cos(th16) - 1.0
    ni = jnp.exp(a16) * jnp.sin(th16)
    den = lr * lr + li * li
    cr = (nr * lr + ni * li) / den
    ci = (ni * lr - nr * li) / den
    bre = bre_ref[...]
    bim = bim_ref[...]
    bbr_ref[...] = cr * bre - ci * bim
    bbi_ref[...] = cr * bim + ci * bre


def _s5prep(lam_re, lam_im, log_dt, b_re, b_im, L):
    dt = jnp.exp(log_dt)[:, None]
    a_row = (lam_re * dt).reshape(1, N_SSM)
    th_row = (lam_im * dt).reshape(1, N_SSM)
    rep = lambda x: jnp.broadcast_to(x.reshape(N_SSM, 1), (N_SSM, SSM_GROUP))
    outs = pl.pallas_call(
        functools.partial(_s5prep_kernel, L=L),
        out_shape=[jax.ShapeDtypeStruct((L, N_SSM), F32)] * 4
        + [jax.ShapeDtypeStruct((1, N_SSM), F32)] * 2
        + [jax.ShapeDtypeStruct((N_SSM, SSM_GROUP), F32)] * 2,
        name="s5prep",
    )(a_row, th_row, rep(lam_re), rep(lam_im), b_re.reshape(N_SSM, SSM_GROUP), b_im.reshape(N_SSM, SSM_GROUP),
      rep(lam_re * dt), rep(lam_im * dt))
    return outs


def _s5_kernel(u_ref, z_ref, h0r_ref, h0i_ref, pnr_ref, pni_ref, ppr_ref, ppi_ref, l1r_ref, l1i_ref,
               bre_ref, bim_ref, cre_ref, cim_ref, d_ref, gw_ref, gb_ref,
               ob_ref, hr_out, hi_out, hr_scr, hi_scr, hbr_scr, hbi_scr, *, L, TT):
    n = pl.program_id(1)
    nc = pl.num_programs(1)

    @pl.when(n == 0)
    def _():
        hr_scr[...] = h0r_ref[0]
        hi_scr[...] = h0i_ref[0]

    _, _, incl, _ = _tri_masks(L)
    tri = incl.astype(BF16)
    u = u_ref[0]
    ub = u.astype(BF16)
    halves = range(S5_HALVES)
    kh = D_B // S5_HALVES
    nh = N_SSM // S5_HALVES
    ncb = N_SSM // S5_COLS
    bus = []
    for cb in range(ncb):
        a = (cb * S5_COLS) // nh
        lc = cb * S5_COLS - a * nh
        ua = ub[:, a * kh:(a + 1) * kh]
        bus.append((jnp.dot(ua, bre_ref[a, :, lc:lc + S5_COLS], preferred_element_type=F32),
                    jnp.dot(ua, bim_ref[a, :, lc:lc + S5_COLS], preferred_element_type=F32)))
    for cb in range(ncb):
        cs = slice(cb * S5_COLS, (cb + 1) * S5_COLS)
        bur, bui = bus[cb]
        l1r = l1r_ref[:, cs]
        l1i = l1i_ref[:, cs]
        hr0 = hr_scr[:, cs]
        hi0 = hi_scr[:, cs]
        for j in range(TT // L):
            rows = slice(j * L, (j + 1) * L)
            pnr = pnr_ref[:, cs]
            pni = pni_ref[:, cs]
            vr = pnr * bur[rows] - pni * bui[rows]
            vi = pnr * bui[rows] + pni * bur[rows]
            cr = _dot(tri, vr) + (l1r * hr0 - l1i * hi0)
            ci = _dot(tri, vi) + (l1r * hi0 + l1i * hr0)
            ppr = ppr_ref[:, cs]
            ppi = ppi_ref[:, cs]
            hr = ppr * cr - ppi * ci
            hi = ppr * ci + ppi * cr
            hr0 = hr[L - 1:L, :]
            hi0 = hi[L - 1:L, :]
            hbr_scr[rows, cs] = hr.astype(BF16)
            hbi_scr[rows, cs] = hi.astype(BF16)
        hr_scr[:, cs] = hr0
        hi_scr[:, cs] = hi0
    y = jnp.concatenate([jnp.dot(hbr_scr[:, a * nh:(a + 1) * nh], cre_ref[a], preferred_element_type=F32)
                         - jnp.dot(hbi_scr[:, a * nh:(a + 1) * nh], cim_ref[a], preferred_element_type=F32)
                         for a in halves], axis=1) + d_ref[...] * u
    c0 = math.sqrt(2.0 / math.pi)
    y = 0.5 * y * (1.0 + jnp.tanh(c0 * (y + 0.044715 * (y * y * y))))
    y = y * _sigmoid(_dot(y, gw_ref[...]) + gb_ref[...])
    ob_ref[0] = (y * z_ref[0]).astype(BF16)

    @pl.when(n == nc - 1)
    def _():
        hr_out[0] = hr_scr[...]
        hi_out[0] = hi_scr[...]


def _s5(proj3, h0r, h0i, tabs, bbd_re, bbd_im, cbd_re, cbd_im, d_row, gw, gb_row, L):
    b, t, _ = proj3.shape
    tt = min(S5_ROWS, t)
    pnr, pni, ppr, ppi, l1r, l1i = tabs
    full = lambda a: pl.BlockSpec(a.shape, lambda i, n: (0,) * a.ndim)
    state = pl.BlockSpec((1, 1, N_SSM), lambda i, n: (i, 0, 0))
    u_col = COL_BU // D_B
    return pl.pallas_call(
        functools.partial(_s5_kernel, L=L, TT=tt),
        grid=(b, t // tt),
        in_specs=[pl.BlockSpec((1, tt, D_B), lambda i, n: (i, n, u_col)),
                  pl.BlockSpec((1, tt, D_B), lambda i, n: (i, n, u_col + 1)),
                  state, state,
                  full(pnr), full(pni), full(ppr), full(ppi), full(l1r), full(l1i),
                  full(bbd_re), full(bbd_im), full(cbd_re), full(cbd_im), full(d_row), full(gw), full(gb_row)],
        out_specs=[pl.BlockSpec((1, tt, D_B), lambda i, n: (i, n, 0)), state, state],
        out_shape=[jax.ShapeDtypeStruct((b, t, D_B), BF16),
                   jax.ShapeDtypeStruct((b, 1, N_SSM), F32),
                   jax.ShapeDtypeStruct((b, 1, N_SSM), F32)],
        scratch_shapes=[pltpu.VMEM((1, N_SSM), F32), pltpu.VMEM((1, N_SSM), F32),
                        pltpu.VMEM((tt, N_SSM), BF16), pltpu.VMEM((tt, N_SSM), BF16)],
        compiler_params=pltpu.CompilerParams(dimension_semantics=("parallel", "arbitrary"),
                                             vmem_limit_bytes=VMEM_LIMIT),
        name="s5",
    )(proj3, proj3, h0r, h0i, pnr, pni, ppr, ppi, l1r, l1i, bbd_re, bbd_im, cbd_re, cbd_im, d_row, gw, gb_row)


def _ml_kernel(q_ref, k_ref, v_ref, o_ref, z_ref, sm_ref, c0_ref, n0_ref, m0_ref, ib_ref, fb_ref, nrm_ref,
               oc_ref, cout_ref, nout_ref, mout_ref, c_scr, n_scr, m_scr, *, L, BB):
    n = pl.program_id(1)
    nc = pl.num_programs(1)

    @pl.when(n == 0)
    def _():
        for r in range(BB):
            c_scr[r * ML_HEADS:(r + 1) * ML_HEADS] = c0_ref[r]
        n_scr[...] = n0_ref[...]
        m_scr[...] = m0_ref[...]

    _, _, incl, _ = _tri_masks(L)
    lane = lax.broadcasted_iota(jnp.int32, (1, LANES), 1)
    pairs = [(r, h) for r in range(BB) for h in range(ML_HEADS)]
    sl = lambda h: slice(h * ML_DH, (h + 1) * ML_DH)

    b_l, imb_l, imbrow_l, mprev_l, nvec_l = [], [], [], [], []
    for r in range(BB):
        sm = sm_ref[r]
        ipre_all = sm + ib_ref[...]
        xf = sm + fb_ref[...]
        logf_all = jnp.minimum(xf, 0.0) - jnp.log(1.0 + jnp.exp(-jnp.abs(xf)))
        b_all = _cumsum_rows(logf_all)
        imb_all = pltpu.roll(ipre_all, SM_F - SM_I, axis=1) - b_all
        imb_t = _transpose_exact(imb_all)
        m_all = m_scr[r]
        n_all = n_scr[r]
        for h in range(ML_HEADS):
            b_l.append(b_all[:, SM_F + h:SM_F + h + 1])
            imb_l.append(imb_all[:, SM_F + h:SM_F + h + 1])
            imbrow_l.append(imb_t[SM_F + h:SM_F + h + 1, :])
            mprev_l.append(m_all[:, h:h + 1])
            nvec_l.append(n_all[h:h + 1, :])

    q = jnp.stack([q_ref[r, :, sl(h)] for r, h in pairs])
    k = jnp.stack([k_ref[r, :, sl(h)] for r, h in pairs])
    v = jnp.stack([v_ref[r, :, sl(h)] for r, h in pairs])
    b = jnp.stack(b_l)
    imb = jnp.stack(imb_l)
    imb_row = jnp.stack(imbrow_l)
    m_prev = jnp.stack(mprev_l)
    nvec = jnp.stack(nvec_l)
    logw = jnp.where(incl, b + imb_row, -jnp.inf)
    lwm = jnp.max(logw, axis=-1, keepdims=True)
    m_t = jnp.maximum(b + m_prev, lwm)
    w_state = jnp.exp(b + m_prev - m_t)
    w_intra = jnp.exp(logw - m_t) * _bdot_nt(q, k)
    c = c_scr[...]
    num = w_state * _bdot(q, c) + _bdot(w_intra, v)
    den = w_state * jnp.sum(q * nvec, axis=-1, keepdims=True) + jnp.sum(w_intra, axis=-1, keepdims=True)
    hh = num / jnp.maximum(jnp.abs(den), jnp.exp(-m_t))
    m_new = m_t[:, L - 1:L, :]
    b_last = b[:, L - 1:L, :]
    w_keep = jnp.exp(b_last + m_prev - m_new)
    w_end = jnp.exp(imb + (b_last - m_new))
    kw = k * w_end
    c_scr[...] = w_keep * c + _bdot_tn(kw, v)
    n_new = w_keep * nvec + jnp.sum(kw, axis=1, keepdims=True)
    for r in range(BB):
        m_next = m_scr[r]
        for h in range(ML_HEADS):
            idx = r * ML_HEADS + h
            n_scr[r, h:h + 1, :] = n_new[idx]
            m_next = jnp.where(lane == h, m_new[idx], m_next)
            gated = o_ref[r, :, sl(h)] * hh[idx]
            oc_ref[r, :, sl(h)] = (_rms(gated, nrm_ref[...]) * z_ref[r, :, sl(h)]).astype(BF16)
        m_scr[r] = m_next

    @pl.when(n == nc - 1)
    def _():
        for r in range(BB):
            cout_ref[r] = c_scr[r * ML_HEADS:(r + 1) * ML_HEADS]
        nout_ref[...] = n_scr[...]
        mout_ref[...] = m_scr[...]


def _ml(proj3, c0, n0, m0p, ib_row, fb_row, nrm_row, L):
    b, t, _ = proj3.shape
    bb = ML_ROWS
    c_col = COL_CQ // D_C
    col = lambda c: (lambda i, n: (i, n, c))
    fixed2 = lambda i, n: (0, 0)
    cspec = pl.BlockSpec((bb, ML_HEADS, ML_DH, ML_DH), lambda i, n: (i, 0, 0, 0))
    nspec = pl.BlockSpec((bb, ML_HEADS, ML_DH), lambda i, n: (i, 0, 0))
    mspec = pl.BlockSpec((bb, 1, LANES), lambda i, n: (i, 0, 0))
    return pl.pallas_call(
        functools.partial(_ml_kernel, L=L, BB=bb),
        grid=(b // bb, t // L),
        in_specs=[pl.BlockSpec((bb, L, D_C), col(c_col + j)) for j in range(5)]
        + [pl.BlockSpec((bb, L, LANES), col(COL_SMALL // LANES)),
           cspec, nspec, mspec,
           pl.BlockSpec((1, LANES), fixed2), pl.BlockSpec((1, LANES), fixed2), pl.BlockSpec((1, ML_DH), fixed2)],
        out_specs=[pl.BlockSpec((bb, L, D_C), lambda i, n: (i, n, 0)), cspec, nspec, mspec],
        out_shape=[jax.ShapeDtypeStruct((b, t, D_C), BF16),
                   jax.ShapeDtypeStruct((b, ML_HEADS, ML_DH, ML_DH), F32),
                   jax.ShapeDtypeStruct((b, ML_HEADS, ML_DH), F32),
                   jax.ShapeDtypeStruct((b, 1, LANES), F32)],
        scratch_shapes=[pltpu.VMEM((bb * ML_HEADS, ML_DH, ML_DH), F32),
                        pltpu.VMEM((bb, ML_HEADS, ML_DH), F32),
                        pltpu.VMEM((bb, 1, LANES), F32)],
        compiler_params=pltpu.CompilerParams(dimension_semantics=("parallel", "arbitrary"),
                                             vmem_limit_bytes=VMEM_LIMIT),
        name="ml",
    )(proj3, proj3, proj3, proj3, proj3, proj3, c0, n0, m0p, ib_row, fb_row, nrm_row)


def _out_kernel(oa_ref, ob_ref, oc_ref, w_ref, x_ref, g_ref, y_ref):
    mix = jnp.concatenate([oa_ref[...], ob_ref[...], oc_ref[...]], axis=1)
    acc = jnp.dot(mix, w_ref[...], preferred_element_type=F32)
    y_ref[...] = x_ref[...] + _rms(acc, g_ref[...])


def _outproj(oa, ob, oc, w_bf, x2, g_row, tm):
    n = x2.shape[0]
    rows = lambda w: pl.BlockSpec((tm, w), lambda i: (i, 0))
    full = lambda a: pl.BlockSpec(a.shape, lambda i: (0, 0))
    return pl.pallas_call(
        _out_kernel,
        grid=(n // tm,),
        in_specs=[rows(D_A), rows(D_B), rows(D_C), full(w_bf), rows(D_MODEL), full(g_row)],
        out_specs=rows(D_MODEL),
        out_shape=jax.ShapeDtypeStruct((n, D_MODEL), F32),
        compiler_params=pltpu.CompilerParams(dimension_semantics=("parallel",),
                                             vmem_limit_bytes=VMEM_LIMIT),
        name="outproj",
    )(oa, ob, oc, w_bf, x2, g_row)


def _pad_lanes(v, offset):
    return jnp.zeros((1, LANES), F32).at[0, offset:offset + v.shape[0]].set(v.astype(F32))


def _block_diag_halves(blocks):
    g, r, c = blocks.shape
    gh = g // S5_HALVES
    eye = jnp.eye(gh, dtype=blocks.dtype)
    bd = jnp.einsum("agrc,gh->agrhc", blocks.reshape(S5_HALVES, gh, r, c), eye)
    return bd.reshape(S5_HALVES, gh * r, gh * c).astype(BF16)


def _prep_layer_weights(w):
    (norm_pre, norm_post, w_in, dn_conv_w, dn_A_log, dn_dt_bias, dn_norm, ssm_lam_re, ssm_lam_im, ssm_log_dt,
     ssm_B_re, ssm_B_im, ssm_C_re, ssm_C_im, ssm_D, ssm_glu_w, ssm_glu_b, ml_i_bias, ml_f_bias, ml_norm, w_out) = w
    o = 0
    seg = {}
    for name, size in (("qkv", 3 * D_A), ("az", D_A), ("beta", DN_HEADS), ("alpha", DN_HEADS), ("bu", D_B),
                       ("bz", D_B), ("cq", D_C), ("ck", D_C), ("cv", D_C), ("co", D_C), ("cz", D_C),
                       ("ci", ML_HEADS), ("cf", ML_HEADS)):
        seg[name] = w_in[:, o:o + size]
        o += size
    n_small = 2 * DN_HEADS + 2 * ML_HEADS
    w_re = jnp.concatenate(
        [seg[k] for k in ("qkv", "az", "bu", "bz", "cq", "ck", "cv", "co", "cz", "beta", "alpha", "ci", "cf")]
        + [jnp.zeros((D_MODEL, LANES - n_small), w_in.dtype)], axis=1).astype(BF16)
    cbd = lambda c: _block_diag_halves(jnp.swapaxes(c.astype(F32), 1, 2))
    return dict(
        norm_pre=norm_pre.reshape(1, D_MODEL).astype(F32), norm_post=norm_post.reshape(1, D_MODEL).astype(F32),
        w_in=w_re, conv_w=dn_conv_w.astype(F32),
        alog_row=_pad_lanes(dn_A_log, SM_ALPHA), dtb_row=_pad_lanes(dn_dt_bias, SM_ALPHA),
        dn_norm=dn_norm.reshape(1, DN_DV).astype(F32),
        lam_re=ssm_lam_re.astype(F32), lam_im=ssm_lam_im.astype(F32), log_dt=ssm_log_dt.astype(F32),
        b_re=ssm_B_re.astype(F32), b_im=ssm_B_im.astype(F32), cbd_re=cbd(ssm_C_re), cbd_im=cbd(ssm_C_im),
        d_row=ssm_D.reshape(1, D_B).astype(F32), glu_w=ssm_glu_w.astype(BF16),
        glu_b=ssm_glu_b.reshape(1, D_B).astype(F32),
        ib_row=_pad_lanes(ml_i_bias, SM_I), fb_row=_pad_lanes(ml_f_bias, SM_F),
        ml_norm=ml_norm.reshape(1, ML_DH).astype(F32),
        w_out=w_out.astype(BF16))


def _s5_tables(p, L):
    pnr, pni, ppr, ppi, l1r, l1i, bbr, bbi = _s5prep(p["lam_re"], p["lam_im"], p["log_dt"], p["b_re"], p["b_im"], L)
    bbd = lambda bb: _block_diag_halves(jnp.swapaxes(bb.reshape(SSM_GROUPS, SSM_STATE, SSM_GROUP), 1, 2))
    return (pnr, pni, ppr, ppi, l1r, l1i), bbd(bbr), bbd(bbi)


def _layer(x, state, p, s5tab, L):
    conv0, s0, h0r, h0i, c0, n0, m0 = state
    b, t, _ = x.shape
    x2 = x.reshape(b * t, D_MODEL)
    conv0p = jnp.pad(conv0.astype(F32), ((0, 0), (SUBLANES - (CONV_W - 1), 0), (0, 0)))
    proj2, conv_tail = _proj(x2, p["norm_pre"], p["w_in"], p["conv_w"], conv0p, min(PROJ_ROWS, t), t)
    proj3 = proj2.reshape(b, t, DP)

    oa, s_new = _dn(proj3, s0.astype(F32), p["alog_row"], p["dtb_row"], p["dn_norm"], L)
    tabs, bbd_re, bbd_im = s5tab
    ob, hr_new, hi_new = _s5(proj3, h0r.reshape(b, 1, N_SSM).astype(F32), h0i.reshape(b, 1, N_SSM).astype(F32),
                             tabs, bbd_re, bbd_im, p["cbd_re"], p["cbd_im"], p["d_row"], p["glu_w"], p["glu_b"], L)
    m0p = jnp.pad(m0.astype(F32), ((0, 0), (0, LANES - ML_HEADS))).reshape(b, 1, LANES)
    oc, c_new, n_new, m_new = _ml(proj3, c0.astype(F32), n0.astype(F32), m0p, p["ib_row"], p["fb_row"],
                                  p["ml_norm"], L)
    y2 = _outproj(oa.reshape(b * t, D_A), ob.reshape(b * t, D_B), oc.reshape(b * t, D_C),
                  p["w_out"], x2, p["norm_post"], min(OUT_ROWS, b * t))
    new_state = (conv_tail[:, SUBLANES - (CONV_W - 1):, :], s_new,
                 hr_new.reshape(b, SSM_GROUPS, SSM_STATE), hi_new.reshape(b, SSM_GROUPS, SSM_STATE),
                 c_new, n_new, m_new[:, 0, :ML_HEADS])
    return y2.reshape(b, t, D_MODEL), new_state


def kernel(x_prompt, x_sample, state_dn_conv, state_dn_S, state_ssm_re, state_ssm_im, state_ml_C, state_ml_n, state_ml_m, norm_pre, norm_post, w_in, dn_conv_w, dn_A_log, dn_dt_bias, dn_norm, ssm_lam_re, ssm_lam_im, ssm_log_dt, ssm_B_re, ssm_B_im, ssm_C_re, ssm_C_im, ssm_D, ssm_glu_w, ssm_glu_b, ml_i_bias, ml_f_bias, ml_norm, w_out):
    bp, tp, _ = x_prompt.shape
    bs, ts, _ = x_sample.shape
    lp = min(CHUNK, tp)
    ls = min(CHUNK, ts)
    zero_state = (jnp.zeros((bp, CONV_W - 1, 3 * D_A), F32),
                  jnp.zeros((bp, DN_HEADS, DN_DK, DN_DV), F32),
                  jnp.zeros((bp, SSM_GROUPS, SSM_STATE), F32),
                  jnp.zeros((bp, SSM_GROUPS, SSM_STATE), F32),
                  jnp.zeros((bp, ML_HEADS, ML_DH, ML_DH), F32),
                  jnp.zeros((bp, ML_HEADS, ML_DH), F32),
                  jnp.zeros((bp, ML_HEADS), F32))
    weights = (norm_pre, norm_post, w_in, dn_conv_w, dn_A_log, dn_dt_bias, dn_norm, ssm_lam_re, ssm_lam_im,
               ssm_log_dt, ssm_B_re, ssm_B_im, ssm_C_re, ssm_C_im, ssm_D, ssm_glu_w, ssm_glu_b, ml_i_bias,
               ml_f_bias, ml_norm, w_out)
    xp, xs = x_prompt, x_sample
    new_p, new_s = [], []
    for l in range(DEPTH):
        p = _prep_layer_weights(tuple(w[l] for w in weights))
        (pnr, pni, ppr, ppi, l1r, l1i), bbd_re, bbd_im = _s5_tables(p, max(lp, ls))
        tabs = lambda L: ((pnr[:L], pni[:L], ppr[:L], ppi[:L], l1r, l1i), bbd_re, bbd_im)
        xp, sp = _layer(xp, zero_state, p, tabs(lp), lp)
        carried = (state_dn_conv[l], state_dn_S[l], state_ssm_re[l], state_ssm_im[l],
                   state_ml_C[l], state_ml_n[l], state_ml_m[l])
        xs, ss = _layer(xs, carried, p, tabs(ls), ls)
        new_p.append(sp)
        new_s.append(ss)
    stack = lambda states, i, dt: jnp.stack([s[i] for s in states], axis=0).astype(dt)
    pst = [stack(new_p, i, x_prompt.dtype) for i in range(7)]
    sst = [stack(new_s, i, x_sample.dtype) for i in range(7)]
    return (xp, xs, pst[0], pst[1], pst[2], pst[3], pst[4], pst[5], pst[6],
            sst[0], sst[1], sst[2], sst[3], sst[4], sst[5], sst[6])
```
